```python
import jax, jax.numpy as jnp
from jax import lax
import numpy as np

D_MODEL = 1024
BATCH = 8
SEQ = 2048
DEPTH = 2
DEC_BATCH = 128
DEC_SEQ = 8
PAST_LEN = 2048
PAGE_SIZE = 128

N_HEADS = 8
HEAD_DIM = D_MODEL // 16
ATTN_WIDTH = N_HEADS * HEAD_DIM
FORGET_BIAS_INIT = 3.0
Q_BLOCK = 128
SC_WIDTH = D_MODEL // 4
SC_KERNEL = 3
CF_WIDTH = D_MODEL // 4
CF_KERNEL = 31
FFN_HIDDEN = -(-8 * D_MODEL // (3 * 256)) * 256
N_BRANCHES = 3
EPS = 1e-6

OFF_Q = 0
OFF_K = OFF_Q + ATTN_WIDTH
OFF_V = OFF_K + ATTN_WIDTH
OFF_F = OFF_V + ATTN_WIDTH
OFF_SB = OFF_F + N_HEADS
OFF_SC = OFF_SB + SC_WIDTH
OFF_SH = OFF_SC + SC_WIDTH
OFF_GLU = OFF_SH + SC_WIDTH
OFF_GATE = OFF_GLU + 2 * CF_WIDTH
P_TOTAL = OFF_GATE + N_BRANCHES * D_MODEL

kernel_name = "hybrid_conv_conformer_fox_decode_step"


def rms_norm(x, g):
    x32 = x.astype(jnp.float32)
    y = x32 * lax.rsqrt(jnp.mean(x32 * x32, axis=-1, keepdims=True) + EPS)
    return (y * g.astype(jnp.float32)).astype(x.dtype)


def layer_norm(x, g, b):
    x32 = x.astype(jnp.float32)
    mu = jnp.mean(x32, axis=-1, keepdims=True)
    xc = x32 - mu
    var = jnp.mean(xc * xc, axis=-1, keepdims=True)
    y = xc * lax.rsqrt(var + EPS) * g.astype(jnp.float32) + b.astype(jnp.float32)
    return y.astype(x.dtype)


def causal_depthwise_conv(u, buf, w):
    width = w.shape[0]
    full = jnp.concatenate([buf.astype(u.dtype), u], axis=1)
    out = lax.conv_general_dilated(
        full, w[:, None, :].astype(u.dtype), window_strides=(1,), padding='VALID',
        dimension_numbers=('NWC', 'WIO', 'NWC'), feature_group_count=u.shape[-1])
    new_buf = full[:, full.shape[1] - (width - 1):]
    return out, new_buf


def forgetting_attention(q, k, v, logf):
    B, Tq, H, Dh = q.shape
    Tk = k.shape[1]
    c = jnp.cumsum(logf.astype(jnp.float32), axis=1)
    ckT = c.transpose(0, 2, 1)
    cq = c[:, Tk - Tq:]
    qb = Q_BLOCK if Tq % Q_BLOCK == 0 else Tq
    nb = Tq // qb
    q_blocks = (q.astype(jnp.float32) * (Dh ** -0.5)).reshape(B, nb, qb, H, Dh).transpose(1, 0, 2, 3, 4)
    cq_blocks = cq.reshape(B, nb, qb, H).transpose(1, 0, 3, 2)
    qpos = (Tk - Tq + jnp.arange(Tq)).reshape(nb, qb)
    kpos = jnp.arange(Tk)
    k32 = k.astype(jnp.float32)
    v32 = v.astype(jnp.float32)

    def block(args):
        qblk, cqblk, qp = args
        s = jnp.einsum('bqhd,bkhd->bhqk', qblk, k32)
        s = s + cqblk[..., None] - ckT[:, :, None, :]
        s = jnp.where(kpos[None, None, None, :] <= qp[None, None, :, None], s, -jnp.inf)
        p = jax.nn.softmax(s, axis=-1)
        return jnp.einsum('bhqk,bkhd->bqhd', p, v32)

    o = lax.map(block, (q_blocks, cq_blocks, qpos))
    return o.transpose(1, 0, 2, 3, 4).reshape(B, Tq, H * Dh).astype(q.dtype)


def hybrid_layer(x, past_k, past_v, past_logf, buf_a, buf_b, w):
    B, T, _ = x.shape
    h = rms_norm(x, w['norm1_g'])
    p = h @ w['w_in']
    q = p[..., OFF_Q:OFF_Q + ATTN_WIDTH].reshape(B, T, N_HEADS, HEAD_DIM)
    k = p[..., OFF_K:OFF_K + ATTN_WIDTH].reshape(B, T, N_HEADS, HEAD_DIM)
    v = p[..., OFF_V:OFF_V + ATTN_WIDTH].reshape(B, T, N_HEADS, HEAD_DIM)
    logf = jax.nn.log_sigmoid(p[..., OFF_F:OFF_F + N_HEADS].astype(jnp.float32)
                              + w['b_f'].astype(jnp.float32))
    k_all = jnp.concatenate([past_k.astype(k.dtype), k], axis=1)
    v_all = jnp.concatenate([past_v.astype(v.dtype), v], axis=1)
    logf_all = jnp.concatenate([past_logf.astype(jnp.float32), logf], axis=1)
    y_c = forgetting_attention(q, k_all, v_all, logf_all) @ w['w_c_out']
    sb = p[..., OFF_SB:OFF_SB + SC_WIDTH]
    sc = p[..., OFF_SC:OFF_SC + SC_WIDTH]
    sh = p[..., OFF_SH:OFF_SH + SC_WIDTH]
    z_a, new_buf_a = causal_depthwise_conv(sc * sh, buf_a, w['conv_a_w'])
    y_a = (sb * z_a) @ w['w_a_out']
    glu_a = p[..., OFF_GLU:OFF_GLU + CF_WIDTH]
    glu_b = p[..., OFF_GLU + CF_WIDTH:OFF_GLU + 2 * CF_WIDTH]
    z_b, new_buf_b = causal_depthwise_conv(glu_a * jax.nn.sigmoid(glu_b), buf_b, w['conv_b_w'])
    z_b = layer_norm(z_b + w['conv_b_bias'].astype(z_b.dtype), w['cf_norm_g'], w['cf_norm_b'])
    y_b = jax.nn.silu(z_b) @ w['w_b_out']
    gates = jax.nn.sigmoid(p[..., OFF_GATE:].reshape(B, T, N_BRANCHES, D_MODEL))
    merged = gates[:, :, 0] * y_a + gates[:, :, 1] * y_b + gates[:, :, 2] * y_c
    x = x + merged @ w['w_o']
    h2 = rms_norm(x, w['norm2_g'])
    x = x + (jax.nn.silu(h2 @ w['w_ffn_gate']) * (h2 @ w['w_ffn_up'])) @ w['w_ffn_down']
    return x, k, v, logf, new_buf_a, new_buf_b


def setup_inputs(seed: int = 0) -> dict:
    key = jax.random.key(seed)
    ks = jax.random.split(key, 32)
    f32 = jnp.float32
    n_pages = PAST_LEN // PAGE_SIZE
    n_used = DEC_BATCH * n_pages
    n_pool = n_used + max(1, n_used // 4)

    def nrm(k, shape, scale):
        return jax.random.normal(k, shape, f32) * scale

    return {
        'x_prompt': nrm(ks[0], (BATCH, SEQ, D_MODEL), 1.0),
        'x_sample': nrm(ks[1], (DEC_BATCH, DEC_SEQ, D_MODEL), 1.0),
        'cache_k': nrm(ks[2], (DEPTH, n_pool, PAGE_SIZE, N_HEADS, HEAD_DIM), 1.0),
        'cache_v': nrm(ks[3], (DEPTH, n_pool, PAGE_SIZE, N_HEADS, HEAD_DIM), 1.0),
        'cache_logf': jax.nn.log_sigmoid(FORGET_BIAS_INIT + nrm(ks[4], (DEPTH, n_pool, PAGE_SIZE, N_HEADS), 1.0)),
        'state_conv_a': nrm(ks[5], (DEPTH, DEC_BATCH, SC_KERNEL - 1, SC_WIDTH), 1.0),
        'state_conv_b': nrm(ks[6], (DEPTH, DEC_BATCH, CF_KERNEL - 1, CF_WIDTH), 1.0),
        'page_table': jax.random.permutation(ks[7], n_pool)[:n_used].reshape(DEC_BATCH, n_pages).astype(jnp.int32),
        'norm1_g': 1.0 + nrm(ks[8], (DEPTH, D_MODEL), 0.02),
        'w_in': nrm(ks[9], (DEPTH, D_MODEL, P_TOTAL), D_MODEL ** -0.5),
        'b_f': FORGET_BIAS_INIT + nrm(ks[10], (DEPTH, N_HEADS), 0.1),
        'conv_a_w': nrm(ks[11], (DEPTH, SC_KERNEL, SC_WIDTH), SC_KERNEL ** -0.5),
        'conv_b_w': nrm(ks[12], (DEPTH, CF_KERNEL, CF_WIDTH), CF_KERNEL ** -0.5),
        'conv_b_bias': nrm(ks[13], (DEPTH, CF_WIDTH), 0.02),
        'cf_norm_g': 1.0 + nrm(ks[14], (DEPTH, CF_WIDTH), 0.02),
        'cf_norm_b': nrm(ks[15], (DEPTH, CF_WIDTH), 0.02),
        'w_a_out': nrm(ks[16], (DEPTH, SC_WIDTH, D_MODEL), SC_WIDTH ** -0.5),
        'w_b_out': nrm(ks[17], (DEPTH, CF_WIDTH, D_MODEL), CF_WIDTH ** -0.5),
        'w_c_out': nrm(ks[18], (DEPTH, ATTN_WIDTH, D_MODEL), ATTN_WIDTH ** -0.5),
        'w_o': nrm(ks[19], (DEPTH, D_MODEL, D_MODEL), D_MODEL ** -0.5),
        'norm2_g': 1.0 + nrm(ks[20], (DEPTH, D_MODEL), 0.02),
        'w_ffn_gate': nrm(ks[21], (DEPTH, D_MODEL, FFN_HIDDEN), D_MODEL ** -0.5),
        'w_ffn_up': nrm(ks[22], (DEPTH, D_MODEL, FFN_HIDDEN), D_MODEL ** -0.5),
        'w_ffn_down': nrm(ks[23], (DEPTH, FFN_HIDDEN, D_MODEL), FFN_HIDDEN ** -0.5),
        'final_norm_g': 1.0 + nrm(ks[24], (D_MODEL,), 0.02),
    }


def reference(x_prompt, x_sample, cache_k, cache_v, cache_logf, state_conv_a, state_conv_b, page_table,
              norm1_g, w_in, b_f, conv_a_w, conv_b_w, conv_b_bias, cf_norm_g, cf_norm_b,
              w_a_out, w_b_out, w_c_out, w_o, norm2_g, w_ffn_gate, w_ffn_up, w_ffn_down, final_norm_g):
    dec_b, n_pages = page_table.shape
    past_len = n_pages * PAGE_SIZE
    pb, ps, _ = x_prompt.shape
    dt = x_prompt.dtype
    xp, xs = x_prompt, x_sample
    kp_l, vp_l, fp_l, ap_l, bp_l = [], [], [], [], []
    ks_l, vs_l, fs_l, as_l, bs_l = [], [], [], [], []
    for l in range(DEPTH):
        w = {'norm1_g': norm1_g[l], 'w_in': w_in[l], 'b_f': b_f[l], 'conv_a_w': conv_a_w[l],
             'conv_b_w': conv_b_w[l], 'conv_b_bias': conv_b_bias[l], 'cf_norm_g': cf_norm_g[l],
             'cf_norm_b': cf_norm_b[l], 'w_a_out': w_a_out[l], 'w_b_out': w_b_out[l], 'w_c_out': w_c_out[l],
             'w_o': w_o[l], 'norm2_g': norm2_g[l], 'w_ffn_gate': w_ffn_gate[l], 'w_ffn_up': w_ffn_up[l],
             'w_ffn_down': w_ffn_down[l]}
        xp, k, v, f, ba, bb = hybrid_layer(
            xp, jnp.zeros((pb, 0, N_HEADS, HEAD_DIM), dt), jnp.zeros((pb, 0, N_HEADS, HEAD_DIM), dt),
            jnp.zeros((pb, 0, N_HEADS), jnp.float32), jnp.zeros((pb, SC_KERNEL - 1, SC_WIDTH), dt),
            jnp.zeros((pb, CF_KERNEL - 1, CF_WIDTH), dt), w)
        kp_l.append(k); vp_l.append(v); fp_l.append(f); ap_l.append(ba); bp_l.append(bb)
        pk = cache_k[l][page_table].reshape(dec_b, past_len, N_HEADS, HEAD_DIM)
        pv = cache_v[l][page_table].reshape(dec_b, past_len, N_HEADS, HEAD_DIM)
        pf = cache_logf[l][page_table].reshape(dec_b, past_len, N_HEADS)
        xs, k, v, f, ba, bb = hybrid_layer(xs, pk, pv, pf, state_conv_a[l], state_conv_b[l], w)
        ks_l.append(k); vs_l.append(v); fs_l.append(f); as_l.append(ba); bs_l.append(bb)
    y_prompt = rms_norm(xp, final_norm_g)
    y_sample = rms_norm(xs, final_norm_g)
    n_pp = ps // PAGE_SIZE
    k_prompt = jnp.stack(kp_l).reshape(DEPTH, pb, n_pp, PAGE_SIZE, N_HEADS, HEAD_DIM)
    v_prompt = jnp.stack(vp_l).reshape(DEPTH, pb, n_pp, PAGE_SIZE, N_HEADS, HEAD_DIM)
    logf_prompt = jnp.stack(fp_l).reshape(DEPTH, pb, n_pp, PAGE_SIZE, N_HEADS)
    conv_a_prompt = jnp.stack(ap_l)
    conv_b_prompt = jnp.stack(bp_l)
    k_sample = jnp.stack(ks_l)
    v_sample = jnp.stack(vs_l)
    logf_sample = jnp.stack(fs_l)
    conv_a_sample = jnp.stack(as_l)
    conv_b_sample = jnp.stack(bs_l)
    return (y_prompt, y_sample, k_prompt, v_prompt, logf_prompt, conv_a_prompt, conv_b_prompt,
            k_sample, v_sample, logf_sample, conv_a_sample, conv_b_sample)
```

```python
import functools

import jax
import jax.numpy as jnp
from jax import lax
from jax.experimental import pallas as pl
from jax.experimental.pallas import tpu as pltpu

F32 = jnp.float32
BF16 = jnp.bfloat16

EPS = 1e-6
N_HEADS = 8
HEAD_DIM = 64
ATTN_WIDTH = N_HEADS * HEAD_DIM
PAGE = 128
SC_WIDTH = 256
SC_KERNEL = 3
CF_WIDTH = 256
CF_KERNEL = 31
N_BRANCHES = 3
LANES = 128
VMEM_LIMIT = 56 * 1024 * 1024


def _nt(a, b):
    return lax.dot_general(a, b, (((1,), (1,)), ((), ())), preferred_element_type=F32)


def _nn(a, b):
    return jnp.dot(a, b, preferred_element_type=F32)


def _split3(x):
    hi = x.astype(BF16)
    r1 = x - hi.astype(F32)
    mid = r1.astype(BF16)
    lo = (r1 - mid.astype(F32)).astype(BF16)
    return hi, mid, lo


def _exact_nn(x, w01):
    hi, mid, lo = _split3(x)
    return _nn(hi, w01) + _nn(mid, w01) + _nn(lo, w01)


def _sigmoid(x):
    return 1.0 / (1.0 + jnp.exp(-x))


def _log_sigmoid(x):
    return jnp.minimum(x, 0.0) - jnp.log(1.0 + jnp.exp(-jnp.abs(x)))


def _rms(x, g):
    return x * lax.rsqrt(jnp.mean(x * x, axis=-1, keepdims=True) + EPS) * g


def _const_spec(shape):
    nd = len(shape)
    return pl.BlockSpec(shape, lambda *_: (0,) * nd, pipeline_mode=pl.Buffered(1))


def _inproj_body(x_ref, g_ref, wq_ref, wk_ref, wv_ref, wf_ref, bf_ref, wmix_ref, wgate_ref,
                 q_ref, k_ref, v_ref, kb_ref, vb_ref, lf_ref, sb_ref, ua_ref, ub_ref, gate_ref, *, paged):
    h = _rms(x_ref[...], g_ref[...]).astype(BF16)
    tm = h.shape[0]

    def mix(c0, c1):
        return _nn(h, wmix_ref[:, c0:c1])

    f = _nn(h, wf_ref[...]) + bf_ref[...]
    lf = _log_sigmoid(f)
    if paged:
        q_ref[...] = (_nn(h, wq_ref[...]) * (HEAD_DIM ** -0.5)).astype(BF16)
        kt = _nt(wk_ref[...], h)
        vt = _nt(wv_ref[...], h)
        lft = lf.T
        for p in range(tm // PAGE):
            sl = slice(p * PAGE, (p + 1) * PAGE)
            k_ref[p] = kt[:, sl]
            v_ref[p] = vt[:, sl]
            kb_ref[p] = kt[:, sl].astype(BF16)
            vb_ref[p] = vt[:, sl].astype(BF16)
            lf_ref[p] = lft[:N_HEADS, sl]
    else:
        q_ref[...] = _nn(h, wq_ref[...]) * (HEAD_DIM ** -0.5)
        k_ref[...] = _nn(h, wk_ref[...])
        v_ref[...] = _nn(h, wv_ref[...])
        lf_ref[...] = lf
    sb_ref[...] = mix(0, SC_WIDTH).astype(BF16)
    ua_ref[...] = mix(SC_WIDTH, 2 * SC_WIDTH) * mix(2 * SC_WIDTH, 3 * SC_WIDTH)
    c0 = 3 * SC_WIDTH
    ub_ref[...] = mix(c0, c0 + CF_WIDTH) * _sigmoid(mix(c0 + CF_WIDTH, c0 + 2 * CF_WIDTH))
    gw = 512
    for c in range(0, gate_ref.shape[1], gw):
        gate_ref[:, c:c + gw] = _sigmoid(_nn(h, wgate_ref[:, c:c + gw])).astype(BF16)


def _inproj(x2d, w, *, paged, tm):
    n, d = x2d.shape
    assert n % tm == 0 and tm % PAGE == 0
    grid = (n // tm,)
    row = lambda width, dt: (jax.ShapeDtypeStruct((n, width), dt), pl.BlockSpec((tm, width), lambda i: (i, 0)))
    if paged:
        npg = tm // PAGE
        pg = lambda rows, dt: (jax.ShapeDtypeStruct((n // PAGE, rows, PAGE), dt),
                               pl.BlockSpec((npg, rows, PAGE), lambda i: (i, 0, 0)))
        outs = [row(ATTN_WIDTH, BF16), pg(ATTN_WIDTH, F32), pg(ATTN_WIDTH, F32), pg(ATTN_WIDTH, BF16),
                pg(ATTN_WIDTH, BF16), pg(N_HEADS, F32)]
        wk, wv = w['wk_t'], w['wv_t']
    else:
        outs = [row(ATTN_WIDTH, F32), row(ATTN_WIDTH, F32), row(ATTN_WIDTH, F32), None, None, row(LANES, F32)]
        wk, wv = w['wk'], w['wv']
    outs += [row(SC_WIDTH, BF16), row(SC_WIDTH, F32), row(CF_WIDTH, F32), row(N_BRANCHES * d, BF16)]
    live = [o for o in outs if o is not None]

    def body(*refs):
        ins, out_refs = refs[:9], list(refs[9:])
        full = [out_refs.pop(0) if o is not None else None for o in outs]
        _inproj_body(*ins, *full, paged=paged)

    weights = [w['norm1_g'], w['wq'], wk, wv, w['wf'], w['bf'], w['wmix'], w['wgate']]
    res = pl.pallas_call(
        body,
        grid=grid,
        in_specs=[pl.BlockSpec((tm, d), lambda i: (i, 0))] + [_const_spec(a.shape) for a in weights],
        out_specs=[o[1] for o in live],
        out_shape=[o[0] for o in live],
        compiler_params=pltpu.CompilerParams(dimension_semantics=("arbitrary",), vmem_limit_bytes=VMEM_LIMIT),
        name="inproj_paged" if paged else "inproj_rows",
    )(x2d, *weights)
    res = list(res)
    return [res.pop(0) if o is not None else None for o in outs]


SUBLANES = 8
A_PAD = 8
B_PAD = 32
A_WIN = 8
B_WIN = 40


def _mixers_body(ua_ref, ub_ref, sb_ref, ha_ref, hb_ref, wa_ref, wb_ref, bias_ref, lng_ref, lnb_ref,
                 za_ref, zb_ref, na_ref, nb_ref, sa, sbuf, *, rows):
    bb, t, _ = ua_ref.shape
    a0 = A_PAD - (SC_KERNEL - 1)
    b0 = B_PAD - (CF_KERNEL - 1)
    sa[:, a0:A_PAD, :] = ha_ref[...]
    sa[:, A_PAD:A_PAD + t, :] = ua_ref[...]
    sbuf[:, b0:B_PAD, :] = hb_ref[...]
    sbuf[:, B_PAD:B_PAD + t, :] = ub_ref[...]
    sbuf[:, B_PAD + t:, :] = jnp.zeros((bb, B_WIN - B_PAD, CF_WIDTH), F32)
    sa[:, 0:a0, :] = jnp.zeros((bb, a0, SC_WIDTH), F32)
    sbuf[:, 0:b0, :] = jnp.zeros((bb, b0, CF_WIDTH), F32)
    na_ref[...] = sa[:, t + a0:t + A_PAD, :]
    nb_ref[...] = sbuf[:, t + b0:t + B_PAD, :]
    bias, lng, lnb = bias_ref[...], lng_ref[...], lnb_ref[...]

    def chunk(r0):
        win = sbuf[:, pl.ds(r0, rows + B_WIN), :]
        acc = None
        for b in range(SUBLANES):
            n_a = (CF_KERNEL - 1 - b) // SUBLANES + 1
            shifted = win[:, b0 + b:b0 + b + rows + SUBLANES * (n_a - 1), :]
            for a in range(n_a):
                i = SUBLANES * a + b
                term = wb_ref[i:i + 1, :] * shifted[:, SUBLANES * a:SUBLANES * a + rows, :]
                acc = term if acc is None else acc + term
        z = acc + bias
        mu = jnp.mean(z, axis=-1, keepdims=True)
        zc = z - mu
        var = jnp.mean(zc * zc, axis=-1, keepdims=True)
        y = zc * lax.rsqrt(var + EPS) * lng + lnb
        zb_ref[:, pl.ds(r0, rows), :] = (y * _sigmoid(y)).astype(BF16)
        win_a = sa[:, pl.ds(r0, rows + A_WIN), :]
        acc_a = wa_ref[0:1, :] * win_a[:, a0:a0 + rows, :]
        for i in range(1, SC_KERNEL):
            acc_a = acc_a + wa_ref[i:i + 1, :] * win_a[:, a0 + i:a0 + i + rows, :]
        za_ref[:, pl.ds(r0, rows), :] = (sb_ref[:, pl.ds(r0, rows), :].astype(F32) * acc_a).astype(BF16)

    if t == rows:
        chunk(0)
    else:
        def step(c, carry):
            chunk(pl.multiple_of(c * rows, rows))
            return carry
        lax.fori_loop(0, t // rows, step, 0)


def _mixers(ua, ub, sb, hist_a, hist_b, w, *, bb, rows):
    b, t, _ = ua.shape
    assert b % bb == 0 and t % rows == 0
    seq = lambda width: pl.BlockSpec((bb, t, width), lambda i: (i, 0, 0))
    hist = lambda r, width: pl.BlockSpec((bb, r, width), lambda i: (i, 0, 0))
    consts = [w['conv_a_w'], w['conv_b_w'], w['conv_b_bias'], w['cf_norm_g'], w['cf_norm_b']]
    return pl.pallas_call(
        functools.partial(_mixers_body, rows=rows),
        grid=(b // bb,),
        in_specs=[seq(SC_WIDTH), seq(CF_WIDTH), seq(SC_WIDTH), hist(SC_KERNEL - 1, SC_WIDTH),
                  hist(CF_KERNEL - 1, CF_WIDTH)] + [_const_spec(a.shape) for a in consts],
        out_specs=[seq(SC_WIDTH), seq(CF_WIDTH), hist(SC_KERNEL - 1, SC_WIDTH), hist(CF_KERNEL - 1, CF_WIDTH)],
        out_shape=[jax.ShapeDtypeStruct((b, t, SC_WIDTH), BF16), jax.ShapeDtypeStruct((b, t, CF_WIDTH), BF16),
                   jax.ShapeDtypeStruct((b, SC_KERNEL - 1, SC_WIDTH), F32),
                   jax.ShapeDtypeStruct((b, CF_KERNEL - 1, CF_WIDTH), F32)],
        scratch_shapes=[pltpu.VMEM((bb, t + A_WIN, SC_WIDTH), F32), pltpu.VMEM((bb, t + B_WIN, CF_WIDTH), F32)],
        compiler_params=pltpu.CompilerParams(dimension_semantics=("arbitrary",), vmem_limit_bytes=VMEM_LIMIT),
        name=f"mixers_t{t}",
    )(ua, ub, sb, hist_a, hist_b, *consts)


def _tri(n, fn):
    r = lax.broadcasted_iota(jnp.int32, (n, n), 0)
    c = lax.broadcasted_iota(jnp.int32, (n, n), 1)
    return jnp.where(fn(r, c), 1.0, 0.0).astype(BF16)


def _prompt_attn_body(q_ref, kt_ref, vt_ref, lft_ref, o_ref, ct_scr, ccol_scr, *, tq):
    npages = kt_ref.shape[0]
    t = npages * PAGE
    ppb = tq // PAGE
    x = lft_ref[...].reshape(npages * N_HEADS, PAGE)
    cw = _exact_nn(x, _tri(PAGE, lambda r, c: r <= c))
    run = jnp.zeros((N_HEADS, 1), F32)
    zpad = jnp.zeros((LANES - N_HEADS, PAGE), F32)
    for p in range(npages):
        cp = cw[p * N_HEADS:(p + 1) * N_HEADS, :] + run
        run = cp[:, PAGE - 1:PAGE]
        ct_scr[p] = cp
        ccol_scr[p * PAGE:(p + 1) * PAGE, :] = jnp.concatenate([cp, zpad], axis=0).T
    zk = jnp.zeros((HEAD_DIM, tq), BF16)
    row_i = lax.broadcasted_iota(jnp.int32, (tq, tq), 0)
    col_i = lax.broadcasted_iota(jnp.int32, (tq, tq), 1)
    lane_lo = lax.broadcasted_iota(jnp.int32, (tq, 2 * HEAD_DIM), 1) < HEAD_DIM

    for hp in range(N_HEADS // 2):
        r_e = slice(hp * 2 * HEAD_DIM, hp * 2 * HEAD_DIM + HEAD_DIM)
        r_o = slice(hp * 2 * HEAD_DIM + HEAD_DIM, (hp + 1) * 2 * HEAD_DIM)
        cols = slice(hp * 2 * HEAD_DIM, (hp + 1) * 2 * HEAD_DIM)

        def kv_block(ref, kj, rows):
            return jnp.concatenate([ref[kj * ppb + j, rows, :] for j in range(ppb)], axis=1)

        def ck_block(kj, h):
            return jnp.concatenate([ct_scr[kj * ppb + j, h:h + 1, :] for j in range(ppb)], axis=1)

        def q_block(qi, carry):
            q0 = pl.multiple_of(qi * tq, tq)
            q2 = q_ref[pl.ds(q0, tq), cols]
            cq = [ccol_scr[pl.ds(q0, tq), 2 * hp + e:2 * hp + e + 1] for e in range(2)]

            def k_step(kj, st, masked):
                m, l, acc = st
                kt = [jnp.concatenate([kv_block(kt_ref, kj, r_e), zk], axis=0),
                      jnp.concatenate([zk, kv_block(kt_ref, kj, r_o)], axis=0)]
                vt = [jnp.concatenate([kv_block(vt_ref, kj, r_e), zk], axis=0),
                      jnp.concatenate([zk, kv_block(vt_ref, kj, r_o)], axis=0)]
                m_new, l_new, alpha, pv = [], [], [], []
                for e in range(2):
                    s = _nn(q2, kt[e]) + cq[e] - ck_block(kj, 2 * hp + e)
                    if masked:
                        s = jnp.where(col_i <= row_i, s, -jnp.inf)
                    mn = jnp.maximum(m[e], jnp.max(s, axis=-1, keepdims=True))
                    a = jnp.exp(m[e] - mn)
                    pe = jnp.exp(s - mn)
                    m_new.append(mn)
                    alpha.append(a)
                    l_new.append(a * l[e] + jnp.sum(pe, axis=-1, keepdims=True))
                    pv.append(_nt(pe.astype(BF16), vt[e]))
                acc = acc * jnp.where(lane_lo, alpha[0], alpha[1]) + pv[0] + pv[1]
                return tuple(m_new), tuple(l_new), acc

            neg = jnp.full((tq, 1), -jnp.inf, F32)
            zero = jnp.zeros((tq, 1), F32)
            st = ((neg, neg), (zero, zero), jnp.zeros((tq, 2 * HEAD_DIM), F32))
            st = lax.fori_loop(0, qi, lambda kj, s_: k_step(kj, s_, False), st)
            m, l, acc = k_step(qi, st, True)
            o_ref[pl.ds(q0, tq), cols] = (acc / jnp.where(lane_lo, l[0], l[1])).astype(o_ref.dtype)
            return carry

        lax.fori_loop(0, t // tq, q_block, 0)


def _prompt_attn(q, ktb, vtb, lft, *, batch, tq):
    n, _ = q.shape
    t = n // batch
    npages = t // PAGE
    assert t % tq == 0 and tq % PAGE == 0
    kv_spec = pl.BlockSpec((npages, ATTN_WIDTH, PAGE), lambda b: (b, 0, 0))
    return pl.pallas_call(
        functools.partial(_prompt_attn_body, tq=tq),
        grid=(batch,),
        in_specs=[pl.BlockSpec((t, ATTN_WIDTH), lambda b: (b, 0)), kv_spec, kv_spec,
                  pl.BlockSpec((npages, N_HEADS, PAGE), lambda b: (b, 0, 0))],
        out_specs=pl.BlockSpec((t, ATTN_WIDTH), lambda b: (b, 0)),
        out_shape=jax.ShapeDtypeStruct((n, ATTN_WIDTH), BF16),
        scratch_shapes=[pltpu.VMEM((npages, N_HEADS, PAGE), F32), pltpu.VMEM((t, LANES), F32)],
        compiler_params=pltpu.CompilerParams(dimension_semantics=("arbitrary",), vmem_limit_bytes=VMEM_LIMIT),
        name="prompt_attn",
    )(q, ktb, vtb, lft)


def _sample_attn_body(pt_ref, q_ref, kn_ref, vn_ref, lfn_ref, *rest, npages):
    kt_refs = rest[:npages]
    vt_refs = rest[npages:2 * npages]
    lft_refs = rest[2 * npages:3 * npages]
    o_ref, s_scr, r_scr = rest[3 * npages:]
    tn = q_ref.shape[0]
    hq = N_HEADS * tn
    nblk = npages + 1

    row = lax.broadcasted_iota(jnp.int32, (hq, ATTN_WIDTH), 0)
    lane = lax.broadcasted_iota(jnp.int32, (hq, ATTN_WIDTH), 1)
    own_head = (lane // HEAD_DIM) == (row // tn)
    q = q_ref[...]
    qbd = jnp.where(own_head, jnp.concatenate([q] * N_HEADS, axis=0), 0.0).astype(BF16)

    x = jnp.concatenate([r[...] for r in lft_refs] + [lfn_ref[...]], axis=0)
    rw = _exact_nn(x, _tri(PAGE, lambda r, c: r > c))
    tot = rw[:, 0:1] + x[:, 0:1]
    run = jnp.zeros((N_HEADS, 1), F32)
    for p in reversed(range(nblk)):
        hs = slice(p * N_HEADS, (p + 1) * N_HEADS)
        r_scr[hs, :] = rw[hs, :] + run
        run = run + tot[hs, :]

    def bias_rows(p):
        return jnp.concatenate(
            [jnp.broadcast_to(r_scr[p * N_HEADS + h:p * N_HEADS + h + 1, :], (tn, PAGE)) for h in range(N_HEADS)],
            axis=0)

    row_p = lax.broadcasted_iota(jnp.int32, (hq, PAGE), 0) % tn
    lane_p = lax.broadcasted_iota(jnp.int32, (hq, PAGE), 1)
    b_new = bias_rows(npages)
    s_q = jnp.sum(jnp.where(lane_p == row_p, b_new, 0.0), axis=-1, keepdims=True)
    zrows = jnp.zeros((PAGE - tn, ATTN_WIDTH), F32)
    kn = jnp.concatenate([kn_ref[...], zrows], axis=0).astype(BF16)
    vn = jnp.concatenate([vn_ref[...], zrows], axis=0).astype(BF16)

    m = jnp.full((hq, 1), -jnp.inf, F32)
    for p in range(nblk):
        if p < npages:
            s = _nn(qbd, kt_refs[p][...].astype(BF16)) + bias_rows(p) - s_q
        else:
            s = jnp.where(lane_p <= row_p, _nt(qbd, kn) + b_new - s_q, -jnp.inf)
        s_scr[:, p * PAGE:(p + 1) * PAGE] = s
        m = jnp.maximum(m, jnp.max(s, axis=-1, keepdims=True))
    l = jnp.zeros((hq, 1), F32)
    acc = jnp.zeros((hq, ATTN_WIDTH), F32)
    for p in range(nblk):
        pe = jnp.exp(s_scr[:, p * PAGE:(p + 1) * PAGE] - m)
        l = l + jnp.sum(pe, axis=-1, keepdims=True)
        if p < npages:
            acc = acc + _nt(pe.astype(BF16), vt_refs[p][...].astype(BF16))
        else:
            acc = acc + _nn(pe.astype(BF16), vn)
    o = jnp.where(own_head, acc / l, 0.0)
    out = o[0:tn, :]
    for h in range(1, N_HEADS):
        out = out + o[h * tn:(h + 1) * tn, :]
    o_ref[...] = out


def _sample_attn(layer, page_table_flat, q, kn, vn, lfn, cache_kt, cache_vt, cache_lft, *, npages):
    nseq, tn, _ = q.shape
    new = lambda width: pl.BlockSpec((None, tn, width), lambda b, pt: (b, 0, 0))

    def page(rows, j):
        return pl.BlockSpec((None, None, rows, PAGE), lambda b, pt: (layer, pt[b * npages + j], 0, 0))

    grid_spec = pltpu.PrefetchScalarGridSpec(
        num_scalar_prefetch=1,
        grid=(nseq,),
        in_specs=[new(ATTN_WIDTH), new(ATTN_WIDTH), new(ATTN_WIDTH),
                  pl.BlockSpec((None, N_HEADS, PAGE), lambda b, pt: (b, 0, 0))]
        + [page(ATTN_WIDTH, j) for j in range(npages)] * 2
        + [page(N_HEADS, j) for j in range(npages)],
        out_specs=new(ATTN_WIDTH),
        scratch_shapes=[pltpu.VMEM((N_HEADS * tn, (npages + 1) * PAGE), F32),
                        pltpu.VMEM(((npages + 1) * N_HEADS, PAGE), F32)],
    )
    return pl.pallas_call(
        functools.partial(_sample_attn_body, npages=npages),
        grid_spec=grid_spec,
        out_shape=jax.ShapeDtypeStruct((nseq, tn, ATTN_WIDTH), F32),
        compiler_params=pltpu.CompilerParams(dimension_semantics=("arbitrary",), vmem_limit_bytes=VMEM_LIMIT),
        name="sample_attn",
    )(page_table_flat, q, kn, vn, lfn, *([cache_kt] * npages), *([cache_vt] * npages), *([cache_lft] * npages))


FFN_CHUNK = 256


def _outproj_body(x_ref, za_ref, zb_ref, oc_ref, gate_ref, wa_ref, wb_ref, wc_ref, wo_ref, g2_ref,
                  wg_ref, wu_ref, wd_ref, gf_ref, y_ref, *, final):
    d = x_ref.shape[1]
    ya = _nn(za_ref[...].astype(BF16), wa_ref[...])
    yb = _nn(zb_ref[...].astype(BF16), wb_ref[...])
    yc = _nn(oc_ref[...].astype(BF16), wc_ref[...])
    merged = (gate_ref[:, 0:d].astype(F32) * ya + gate_ref[:, d:2 * d].astype(F32) * yb
              + gate_ref[:, 2 * d:3 * d].astype(F32) * yc)
    x1 = x_ref[...] + _nn(merged.astype(BF16), wo_ref[...])
    h2 = _rms(x1, g2_ref[...]).astype(BF16)
    acc = x1
    for c in range(0, wg_ref.shape[1], FFN_CHUNK):
        gch = _nn(h2, wg_ref[:, c:c + FFN_CHUNK])
        uch = _nn(h2, wu_ref[:, c:c + FFN_CHUNK])
        acc = acc + _nn((gch * _sigmoid(gch) * uch).astype(BF16), wd_ref[c:c + FFN_CHUNK, :])
    y_ref[...] = _rms(acc, gf_ref[...]) if final else acc


def _outproj(x2d, za, zb, oc, gates, w, gf, *, final, tm):
    n, d = x2d.shape
    assert n % tm == 0
    row = lambda a: pl.BlockSpec((tm, a.shape[1]), lambda i: (i, 0))
    weights = [w['wa'], w['wb'], w['wc'], w['wo'], w['norm2_g'], w['wg'], w['wu'], w['wd'], gf]
    return pl.pallas_call(
        functools.partial(_outproj_body, final=final),
        grid=(n // tm,),
        in_specs=[row(x2d), row(za), row(zb), row(oc), row(gates)] + [_const_spec(a.shape) for a in weights],
        out_specs=pl.BlockSpec((tm, d), lambda i: (i, 0)),
        out_shape=jax.ShapeDtypeStruct((n, d), F32),
        compiler_params=pltpu.CompilerParams(dimension_semantics=("arbitrary",), vmem_limit_bytes=VMEM_LIMIT),
        name="outproj_final" if final else "outproj",
    )(x2d, za, zb, oc, gates, *weights)


def _layer_weights(l, norm1_g, w_in, b_f, conv_a_w, conv_b_w, conv_b_bias, cf_norm_g, cf_norm_b,
                   w_a_out, w_b_out, w_c_out, w_o, norm2_g, w_ffn_gate, w_ffn_up, w_ffn_down):
    d = w_in.shape[1]
    wi = w_in[l]
    off_f = 3 * ATTN_WIDTH
    off_mix = off_f + N_HEADS
    off_gate = off_mix + 3 * SC_WIDTH + 2 * CF_WIDTH
    wq, wk, wv = (wi[:, i * ATTN_WIDTH:(i + 1) * ATTN_WIDTH].astype(BF16) for i in range(3))
    pad = LANES - N_HEADS
    return {
        'norm1_g': norm1_g[l].reshape(1, d),
        'wq': wq, 'wk': wk, 'wv': wv, 'wk_t': wk.T, 'wv_t': wv.T,
        'wf': jnp.pad(wi[:, off_f:off_mix], ((0, 0), (0, pad))).astype(BF16),
        'bf': jnp.pad(b_f[l], (0, pad)).reshape(1, LANES),
        'wmix': wi[:, off_mix:off_gate].astype(BF16),
        'wgate': wi[:, off_gate:].astype(BF16),
        'conv_a_w': conv_a_w[l], 'conv_b_w': conv_b_w[l], 'conv_b_bias': conv_b_bias[l].reshape(1, -1),
        'cf_norm_g': cf_norm_g[l].reshape(1, -1), 'cf_norm_b': cf_norm_b[l].reshape(1, -1),
        'wa': w_a_out[l].astype(BF16), 'wb': w_b_out[l].astype(BF16), 'wc': w_c_out[l].astype(BF16),
        'wo': w_o[l].astype(BF16), 'norm2_g': norm2_g[l].reshape(1, d),
        'wg': w_ffn_gate[l].astype(BF16), 'wu': w_ffn_up[l].astype(BF16), 'wd': w_ffn_down[l].astype(BF16),
    }


def kernel(x_prompt, x_sample, cache_k, cache_v, cache_logf, state_conv_a, state_conv_b, page_table, norm1_g, w_in, b_f, conv_a_w, conv_b_w, conv_b_bias, cf_norm_g, cf_norm_b, w_a_out, w_b_out, w_c_out, w_o, norm2_g, w_ffn_gate, w_ffn_up, w_ffn_down, final_norm_g):
    depth = w_in.shape[0]
    pb, ps, d = x_prompt.shape
    sb_, st, _ = x_sample.shape
    n_pool = cache_k.shape[1]
    npages = page_table.shape[1]
    assert cache_k.shape[2:] == (PAGE, N_HEADS, HEAD_DIM) and ps % PAGE == 0
    cache_kt = cache_k.transpose(0, 1, 3, 4, 2).reshape(depth, n_pool, ATTN_WIDTH, PAGE)
    cache_vt = cache_v.transpose(0, 1, 3, 4, 2).reshape(depth, n_pool, ATTN_WIDTH, PAGE)
    cache_lft = cache_logf.transpose(0, 1, 3, 2)
    pt_flat = page_table.reshape(-1)
    gf = final_norm_g.reshape(1, d)
    tm_p = 512 if (pb * ps) % 512 == 0 else PAGE
    tm_s = 512 if (sb_ * st) % 512 == 0 else sb_ * st
    tq = 256 if ps % 256 == 0 else PAGE
    bb = 16 if sb_ % 16 == 0 else sb_
    conv_rows = 64 if ps % 64 == 0 else ps

    xp = x_prompt.reshape(pb * ps, d)
    xs = x_sample.reshape(sb_ * st, d)
    zeros_a = jnp.zeros((pb, SC_KERNEL - 1, SC_WIDTH), F32)
    zeros_b = jnp.zeros((pb, CF_KERNEL - 1, CF_WIDTH), F32)
    outs = {k: [] for k in ('kp', 'vp', 'fp', 'ap', 'bp', 'ks', 'vs', 'fs', 'as', 'bs')}
    for l in range(depth):
        w = _layer_weights(l, norm1_g, w_in, b_f, conv_a_w, conv_b_w, conv_b_bias, cf_norm_g, cf_norm_b,
                           w_a_out, w_b_out, w_c_out, w_o, norm2_g, w_ffn_gate, w_ffn_up, w_ffn_down)
        final = l == depth - 1
        q, kt, vt, ktb, vtb, lft, sbp, ua, ub, gates = _inproj(xp, w, paged=True, tm=tm_p)
        za, zb, na, nb = _mixers(ua.reshape(pb, ps, -1), ub.reshape(pb, ps, -1), sbp.reshape(pb, ps, -1),
                                 zeros_a, zeros_b, w, bb=1, rows=conv_rows)
        oc = _prompt_attn(q, ktb, vtb, lft, batch=pb, tq=tq)
        xp = _outproj(xp, za.reshape(pb * ps, -1), zb.reshape(pb * ps, -1), oc, gates, w, gf, final=final, tm=tm_p)
        npp = ps // PAGE
        to_pages = lambda a: a.reshape(pb, npp, N_HEADS, HEAD_DIM, PAGE).transpose(0, 1, 4, 2, 3)
        outs['kp'].append(to_pages(kt))
        outs['vp'].append(to_pages(vt))
        outs['fp'].append(lft.reshape(pb, npp, N_HEADS, PAGE).transpose(0, 1, 3, 2))
        outs['ap'].append(na)
        outs['bp'].append(nb)
        q, k, v, _, _, lf, sbs, ua, ub, gates = _inproj(xs, w, paged=False, tm=tm_s)
        za, zb, na, nb = _mixers(ua.reshape(sb_, st, -1), ub.reshape(sb_, st, -1), sbs.reshape(sb_, st, -1),
                                 state_conv_a[l], state_conv_b[l], w, bb=bb, rows=st)
        lf = lf[:, :N_HEADS].reshape(sb_, st, N_HEADS)
        lfn = jnp.pad(lf.transpose(0, 2, 1), ((0, 0), (0, 0), (0, PAGE - st)))
        oc = _sample_attn(l, pt_flat, q.reshape(sb_, st, -1), k.reshape(sb_, st, -1), v.reshape(sb_, st, -1), lfn,
                          cache_kt, cache_vt, cache_lft, npages=npages)
        xs = _outproj(xs, za.reshape(sb_ * st, -1), zb.reshape(sb_ * st, -1), oc.reshape(sb_ * st, -1), gates, w, gf,
                      final=final, tm=tm_s)
        outs['ks'].append(k.reshape(sb_, st, N_HEADS, HEAD_DIM))
        outs['vs'].append(v.reshape(sb_, st, N_HEADS, HEAD_DIM))
        outs['fs'].append(lf)
        outs['as'].append(na)
        outs['bs'].append(nb)
    stack = lambda k: jnp.stack(outs[k])
    return (xp.reshape(pb, ps, d), xs.reshape(sb_, st, d), stack('kp'), stack('vp'), stack('fp'), stack('ap'),
            stack('bp'), stack('ks'), stack('vs'), stack('fs'), stack('as'), stack('bs'))
```

```python
import functools

import jax
import jax.numpy as jnp
from jax import lax
from jax.experimental import pallas as pl
from jax.experimental.pallas import tpu as pltpu

F32 = jnp.float32
BF16 = jnp.bfloat16

EPS = 1e-6
LOG2E = 1.4426950408889634
N_HEADS = 8
HEAD_DIM = 64
ATTN_WIDTH = N_HEADS * HEAD_DIM
PAGE = 128
SC_WIDTH = 256
SC_KERNEL = 3
CF_WIDTH = 256
CF_KERNEL = 31
N_BRANCHES = 3
LANES = 128
VMEM_LIMIT = 56 * 1024 * 1024


def _nt(a, b):
    return lax.dot_general(a, b, (((1,), (1,)), ((), ())), preferred_element_type=F32)


def _nn(a, b):
    return jnp.dot(a, b, preferred_element_type=F32)


def _split3(x):
    hi = x.astype(BF16)
    r1 = x - hi.astype(F32)
    mid = r1.astype(BF16)
    lo = (r1 - mid.astype(F32)).astype(BF16)
    return hi, mid, lo


def _exact_nn(x, w01):
    hi, mid, lo = _split3(x)
    return _nn(hi, w01) + _nn(mid, w01) + _nn(lo, w01)


def _sigmoid(x):
    return 1.0 / (1.0 + jnp.exp(-x))


def _log_sigmoid(x):
    return jnp.minimum(x, 0.0) - jnp.log(1.0 + jnp.exp(-jnp.abs(x)))


def _rms(x, g):
    return x * lax.rsqrt(jnp.mean(x * x, axis=-1, keepdims=True) + EPS) * g


def _const_spec(shape):
    nd = len(shape)
    return pl.BlockSpec(shape, lambda *_: (0,) * nd, pipeline_mode=pl.Buffered(1))


def _inproj_body(x_ref, g_ref, wq_ref, wk_ref, wv_ref, wf_ref, bf_ref, wmix_ref, wgate_ref,
                 q_ref, k_ref, v_ref, kb_ref, vb_ref, lf_ref, sb_ref, ua_ref, ub_ref, gate_ref, *, paged):
    h = _rms(x_ref[...], g_ref[...]).astype(BF16)
    tm = h.shape[0]

    def mix(c0, c1):
        return _nn(h, wmix_ref[:, c0:c1])

    f = _nn(h, wf_ref[...]) + bf_ref[...]
    lf = _log_sigmoid(f)
    if paged:
        qt = (_nt(wq_ref[...], h) * (HEAD_DIM ** -0.5 * LOG2E)).astype(BF16)
        kt = _nt(wk_ref[...], h)
        vt = _nt(wv_ref[...], h)
        kb_ref[...] = kt.T.astype(BF16)
        lft = lf.T
        for p in range(tm // PAGE):
            sl = slice(p * PAGE, (p + 1) * PAGE)
            q_ref[p] = qt[:, sl]
            k_ref[p] = kt[:, sl]
            v_ref[p] = vt[:, sl]
            vb_ref[p] = vt[:, sl].astype(BF16)
            lf_ref[p] = lft[:N_HEADS, sl]
    else:
        q_ref[...] = _nn(h, wq_ref[...]) * (HEAD_DIM ** -0.5)
        k_ref[...] = _nn(h, wk_ref[...])
        v_ref[...] = _nn(h, wv_ref[...])
        lf_ref[...] = lf
    sb_ref[...] = mix(0, SC_WIDTH).astype(BF16)
    ua_ref[...] = mix(SC_WIDTH, 2 * SC_WIDTH) * mix(2 * SC_WIDTH, 3 * SC_WIDTH)
    c0 = 3 * SC_WIDTH
    ub_ref[...] = mix(c0, c0 + CF_WIDTH) * _sigmoid(mix(c0 + CF_WIDTH, c0 + 2 * CF_WIDTH))
    gw = 512
    for c in range(0, gate_ref.shape[1], gw):
        gate_ref[:, c:c + gw] = _sigmoid(_nn(h, wgate_ref[:, c:c + gw])).astype(BF16)


def _inproj(x2d, w, *, paged, tm):
    n, d = x2d.shape
    assert n % tm == 0 and tm % PAGE == 0
    grid = (n // tm,)
    row = lambda width, dt: (jax.ShapeDtypeStruct((n, width), dt), pl.BlockSpec((tm, width), lambda i: (i, 0)))
    if paged:
        npg = tm // PAGE
        pg = lambda rows, dt: (jax.ShapeDtypeStruct((n // PAGE, rows, PAGE), dt),
                               pl.BlockSpec((npg, rows, PAGE), lambda i: (i, 0, 0)))
        outs = [pg(ATTN_WIDTH, BF16), pg(ATTN_WIDTH, F32), pg(ATTN_WIDTH, F32), row(ATTN_WIDTH, BF16),
                pg(ATTN_WIDTH, BF16), pg(N_HEADS, F32)]
        wq, wk, wv = w['wq_t'], w['wk_t'], w['wv_t']
    else:
        outs = [row(ATTN_WIDTH, F32), row(ATTN_WIDTH, F32), row(ATTN_WIDTH, F32), None, None, row(LANES, F32)]
        wq, wk, wv = w['wq'], w['wk'], w['wv']
    outs += [row(SC_WIDTH, BF16), row(SC_WIDTH, F32), row(CF_WIDTH, F32), row(N_BRANCHES * d, BF16)]
    live = [o for o in outs if o is not None]

    def body(*refs):
        ins, out_refs = refs[:9], list(refs[9:])
        full = [out_refs.pop(0) if o is not None else None for o in outs]
        _inproj_body(*ins, *full, paged=paged)

    weights = [w['norm1_g'], wq, wk, wv, w['wf'], w['bf'], w['wmix'], w['wgate']]
    res = pl.pallas_call(
        body,
        grid=grid,
        in_specs=[pl.BlockSpec((tm, d), lambda i: (i, 0))] + [_const_spec(a.shape) for a in weights],
        out_specs=[o[1] for o in live],
        out_shape=[o[0] for o in live],
        compiler_params=pltpu.CompilerParams(dimension_semantics=("arbitrary",), vmem_limit_bytes=VMEM_LIMIT),
        name="inproj_paged" if paged else "inproj_rows",
    )(x2d, *weights)
    res = list(res)
    return [res.pop(0) if o is not None else None for o in outs]


SUBLANES = 8
A_PAD = 8
B_PAD = 32
A_WIN = 8
B_WIN = 40


def _shift_rows(x, first, n):
    if first % SUBLANES == 0:
        return x[:, first:first + n, :]
    total = x.shape[1]
    return pltpu.roll(x, total - first, 1)[:, :n, :]


def _mixers_body(ua_ref, ub_ref, sb_ref, ha_ref, hb_ref, wa_ref, wb_ref, bias_ref, lng_ref, lnb_ref,
                 za_ref, zb_ref, na_ref, nb_ref, sa, sbuf, *, rows):
    bb, t, _ = ua_ref.shape
    a0 = A_PAD - (SC_KERNEL - 1)
    b0 = B_PAD - (CF_KERNEL - 1)
    sa[:, a0:A_PAD, :] = ha_ref[...]
    sa[:, A_PAD:A_PAD + t, :] = ua_ref[...]
    sbuf[:, b0:B_PAD, :] = hb_ref[...]
    sbuf[:, B_PAD:B_PAD + t, :] = ub_ref[...]
    sbuf[:, B_PAD + t:, :] = jnp.zeros((bb, B_WIN - B_PAD, CF_WIDTH), F32)
    sa[:, 0:a0, :] = jnp.zeros((bb, a0, SC_WIDTH), F32)
    sbuf[:, 0:b0, :] = jnp.zeros((bb, b0, CF_WIDTH), F32)
    na_ref[...] = sa[:, t + a0:t + A_PAD, :]
    nb_ref[...] = sbuf[:, t + b0:t + B_PAD, :]
    bias, lng, lnb = bias_ref[...], lng_ref[...], lnb_ref[...]

    def chunk(r0):
        win = sbuf[:, pl.ds(r0, rows + B_WIN), :]
        acc = None
        for b in range(SUBLANES):
            n_a = (CF_KERNEL - 1 - b) // SUBLANES + 1
            shifted = _shift_rows(win, b0 + b, rows + SUBLANES * (n_a - 1))
            for a in range(n_a):
                i = SUBLANES * a + b
                term = wb_ref[i:i + 1, :] * shifted[:, SUBLANES * a:SUBLANES * a + rows, :]
                acc = term if acc is None else acc + term
        z = acc + bias
        mu = jnp.mean(z, axis=-1, keepdims=True)
        zc = z - mu
        var = jnp.mean(zc * zc, axis=-1, keepdims=True)
        y = zc * lax.rsqrt(var + EPS) * lng + lnb
        zb_ref[:, pl.ds(r0, rows), :] = (y * _sigmoid(y)).astype(BF16)
        win_a = sa[:, pl.ds(r0, rows + A_WIN), :]
        acc_a = wa_ref[0:1, :] * _shift_rows(win_a, a0, rows)
        for i in range(1, SC_KERNEL):
            acc_a = acc_a + wa_ref[i:i + 1, :] * _shift_rows(win_a, a0 + i, rows)
        za_ref[:, pl.ds(r0, rows), :] = (sb_ref[:, pl.ds(r0, rows), :].astype(F32) * acc_a).astype(BF16)

    if t == rows:
        chunk(0)
    else:
        def step(c, carry):
            chunk(pl.multiple_of(c * rows, rows))
            return carry
        lax.fori_loop(0, t // rows, step, 0)


def _mixers(ua, ub, sb, hist_a, hist_b, w, *, bb, rows):
    b, t, _ = ua.shape
    assert b % bb == 0 and t % rows == 0
    seq = lambda width: pl.BlockSpec((bb, t, width), lambda i: (i, 0, 0))
    hist = lambda r, width: pl.BlockSpec((bb, r, width), lambda i: (i, 0, 0))
    consts = [w['conv_a_w'], w['conv_b_w'], w['conv_b_bias'], w['cf_norm_g'], w['cf_norm_b']]
    return pl.pallas_call(
        functools.partial(_mixers_body, rows=rows),
        grid=(b // bb,),
        in_specs=[seq(SC_WIDTH), seq(CF_WIDTH), seq(SC_WIDTH), hist(SC_KERNEL - 1, SC_WIDTH),
                  hist(CF_KERNEL - 1, CF_WIDTH)] + [_const_spec(a.shape) for a in consts],
        out_specs=[seq(SC_WIDTH), seq(CF_WIDTH), hist(SC_KERNEL - 1, SC_WIDTH), hist(CF_KERNEL - 1, CF_WIDTH)],
        out_shape=[jax.ShapeDtypeStruct((b, t, SC_WIDTH), BF16), jax.ShapeDtypeStruct((b, t, CF_WIDTH), BF16),
                   jax.ShapeDtypeStruct((b, SC_KERNEL - 1, SC_WIDTH), F32),
                   jax.ShapeDtypeStruct((b, CF_KERNEL - 1, CF_WIDTH), F32)],
        scratch_shapes=[pltpu.VMEM((bb, t + A_WIN, SC_WIDTH), F32), pltpu.VMEM((bb, t + B_WIN, CF_WIDTH), F32)],
        compiler_params=pltpu.CompilerParams(dimension_semantics=("arbitrary",), vmem_limit_bytes=VMEM_LIMIT),
        name=f"mixers_t{t}",
    )(ua, ub, sb, hist_a, hist_b, *consts)


def _tri(n, fn):
    r = lax.broadcasted_iota(jnp.int32, (n, n), 0)
    c = lax.broadcasted_iota(jnp.int32, (n, n), 1)
    return jnp.where(fn(r, c), 1.0, 0.0).astype(BF16)


def _prompt_attn_body(qt_ref, k_ref, vt_ref, lft_ref, o_ref, ct_scr, ccol_scr, crep_scr, m_scr, l_scr, acc_scr,
                      *, tq, heads_per_step):
    npages = qt_ref.shape[0]
    t = npages * PAGE
    ppb = tq // PAGE
    x = lft_ref[...].reshape(npages * N_HEADS, PAGE)
    cw = _exact_nn(x, _tri(PAGE, lambda r, c: r <= c))
    run = jnp.zeros((N_HEADS, 1), F32)
    zpad = jnp.zeros((LANES - N_HEADS, PAGE), F32)
    for p in range(npages):
        cp = cw[p * N_HEADS:(p + 1) * N_HEADS, :] + run
        run = cp[:, PAGE - 1:PAGE]
        cp2 = cp * LOG2E
        ct_scr[p] = cp2
        ccol_scr[p * PAGE:(p + 1) * PAGE, :] = jnp.concatenate([cp2, zpad], axis=0).T
    zq = jnp.zeros((HEAD_DIM, tq), BF16)
    key_i = lax.broadcasted_iota(jnp.int32, (tq, tq), 0)
    qry_i = lax.broadcasted_iota(jnp.int32, (tq, tq), 1)
    for h in range(N_HEADS):
        for r in range(0, t, tq):
            crep_scr[h, r:r + tq, :] = jnp.broadcast_to(ccol_scr[r:r + tq, h:h + 1], (tq, LANES))

    def paged(ref, blk, rws):
        return jnp.concatenate([ref[blk * ppb + j, rws, :] for j in range(ppb)], axis=1)

    for h0 in range(0, N_HEADS, heads_per_step):
        heads = range(h0, h0 + heads_per_step)

        def q_block(qi, carry):
            q0 = pl.multiple_of(qi * tq, tq)
            for h in heads:
                m_scr[h] = jnp.full((1, tq), -jnp.inf, F32)
                l_scr[h] = jnp.zeros((1, tq), F32)
                acc_scr[h] = jnp.zeros((HEAD_DIM, tq), F32)

            def k_step(kj, masked):
                k0 = pl.multiple_of(kj * tq, tq)
                s, pe, alpha = {}, {}, {}

                def scores(h):
                    pair = slice((h // 2) * 2 * HEAD_DIM, (h // 2 + 1) * 2 * HEAD_DIM)
                    qh = paged(qt_ref, qi, slice(h * HEAD_DIM, (h + 1) * HEAD_DIM))
                    qtz = jnp.concatenate([qh, zq] if h % 2 == 0 else [zq, qh], axis=0)
                    cs = crep_scr[h, pl.ds(k0, tq), :]
                    sh = _nn(k_ref[pl.ds(k0, tq), pair], qtz) - jnp.concatenate([cs] * (tq // LANES), axis=1)
                    s[h] = jnp.where(key_i <= qry_i, sh, -jnp.inf) if masked else sh

                def softmax(h):
                    ct = jnp.concatenate([ct_scr[qi * ppb + j, h:h + 1, :] for j in range(ppb)], axis=1)
                    m = m_scr[h]
                    m_new = jnp.maximum(m, jnp.max(s[h], axis=0, keepdims=True) + ct)
                    alpha[h] = jnp.exp2(m - m_new)
                    p = jnp.exp2(s.pop(h) + (ct - m_new))
                    m_scr[h] = m_new
                    l_scr[h] = alpha[h] * l_scr[h] + jnp.sum(p, axis=0, keepdims=True)
                    pe[h] = p.astype(BF16)

                def weighted_values(h):
                    vt = paged(vt_ref, kj, slice(h * HEAD_DIM, (h + 1) * HEAD_DIM))
                    acc_scr[h] = acc_scr[h] * alpha.pop(h) + _nn(vt, pe.pop(h))

                hs = list(heads)
                for i in range(len(hs) + 2):
                    if i < len(hs):
                        scores(hs[i])
                    if 0 <= i - 2 < len(hs):
                        weighted_values(hs[i - 2])
                    if 0 <= i - 1 < len(hs):
                        softmax(hs[i - 1])

            def unmasked(kj, c):
                k_step(kj, False)
                return c

            lax.fori_loop(0, qi, unmasked, 0)
            k_step(qi, True)
            for hp in range(h0 // 2, (h0 + heads_per_step) // 2):
                ot = jnp.concatenate([acc_scr[2 * hp + e] / l_scr[2 * hp + e] for e in range(2)], axis=0)
                o_ref[pl.ds(q0, tq), hp * 2 * HEAD_DIM:(hp + 1) * 2 * HEAD_DIM] = ot.T.astype(o_ref.dtype)
            return carry

        lax.fori_loop(0, t // tq, q_block, 0)


ATTN_HEADS_PER_STEP = 8


def _prompt_attn(qt, kb, vtb, lft, *, batch, tq):
    n, _ = kb.shape
    t = n // batch
    npages = t // PAGE
    assert t % tq == 0 and tq % PAGE == 0
    paged_spec = pl.BlockSpec((npages, ATTN_WIDTH, PAGE), lambda b: (b, 0, 0))
    return pl.pallas_call(
        functools.partial(_prompt_attn_body, tq=tq, heads_per_step=ATTN_HEADS_PER_STEP),
        grid=(batch,),
        in_specs=[paged_spec, pl.BlockSpec((t, ATTN_WIDTH), lambda b: (b, 0)), paged_spec,
                  pl.BlockSpec((npages, N_HEADS, PAGE), lambda b: (b, 0, 0))],
        out_specs=pl.BlockSpec((t, ATTN_WIDTH), lambda b: (b, 0)),
        out_shape=jax.ShapeDtypeStruct((n, ATTN_WIDTH), BF16),
        scratch_shapes=[pltpu.VMEM((npages, N_HEADS, PAGE), F32), pltpu.VMEM((t, LANES), F32),
                        pltpu.VMEM((N_HEADS, t, LANES), F32), pltpu.VMEM((N_HEADS, 1, tq), F32),
                        pltpu.VMEM((N_HEADS, 1, tq), F32), pltpu.VMEM((N_HEADS, HEAD_DIM, tq), F32)],
        compiler_params=pltpu.CompilerParams(dimension_semantics=("arbitrary",), vmem_limit_bytes=VMEM_LIMIT),
        name="prompt_attn",
    )(qt, kb, vtb, lft)


def _sample_attn_body(pt_ref, q_ref, kn_ref, vn_ref, lfn_ref, *rest, npages):
    kt_refs = rest[:npages]
    vt_refs = rest[npages:2 * npages]
    lft_refs = rest[2 * npages:3 * npages]
    o_ref, s_scr, r_scr = rest[3 * npages:]
    tn = q_ref.shape[0]
    hq = N_HEADS * tn
    nblk = npages + 1

    row = lax.broadcasted_iota(jnp.int32, (hq, ATTN_WIDTH), 0)
    lane = lax.broadcasted_iota(jnp.int32, (hq, ATTN_WIDTH), 1)
    own_head = (lane // HEAD_DIM) == (row // tn)
    q = q_ref[...]
    qbd = jnp.where(own_head, jnp.concatenate([q] * N_HEADS, axis=0), 0.0).astype(BF16)

    x = jnp.concatenate([r[...] for r in lft_refs] + [lfn_ref[...]], axis=0)
    rw = _exact_nn(x, _tri(PAGE, lambda r, c: r > c))
    tot = rw[:, 0:1] + x[:, 0:1]
    run = jnp.zeros((N_HEADS, 1), F32)
    for p in reversed(range(nblk)):
        hs = slice(p * N_HEADS, (p + 1) * N_HEADS)
        r_scr[hs, :] = rw[hs, :] + run
        run = run + tot[hs, :]

    def bias_rows(p):
        return jnp.concatenate(
            [jnp.broadcast_to(r_scr[p * N_HEADS + h:p * N_HEADS + h + 1, :], (tn, PAGE)) for h in range(N_HEADS)],
            axis=0)

    row_p = lax.broadcasted_iota(jnp.int32, (hq, PAGE), 0) % tn
    lane_p = lax.broadcasted_iota(jnp.int32, (hq, PAGE), 1)
    b_new = bias_rows(npages)
    s_q = jnp.sum(jnp.where(lane_p == row_p, b_new, 0.0), axis=-1, keepdims=True)
    zrows = jnp.zeros((PAGE - tn, ATTN_WIDTH), F32)
    kn = jnp.concatenate([kn_ref[...], zrows], axis=0).astype(BF16)
    vn = jnp.concatenate([vn_ref[...], zrows], axis=0).astype(BF16)

    m = jnp.full((hq, 1), -jnp.inf, F32)
    for p in range(nblk):
        if p < npages:
            s = _nn(qbd, kt_refs[p][...].astype(BF16)) + bias_rows(p) - s_q
        else:
            s = jnp.where(lane_p <= row_p, _nt(qbd, kn) + b_new - s_q, -jnp.inf)
        s_scr[:, p * PAGE:(p + 1) * PAGE] = s
        m = jnp.maximum(m, jnp.max(s, axis=-1, keepdims=True))
    l = jnp.zeros((hq, 1), F32)
    acc = jnp.zeros((hq, ATTN_WIDTH), F32)
    for p in range(nblk):
        pe = jnp.exp(s_scr[:, p * PAGE:(p + 1) * PAGE] - m)
        l = l + jnp.sum(pe, axis=-1, keepdims=True)
        if p < npages:
            acc = acc + _nt(pe.astype(BF16), vt_refs[p][...].astype(BF16))
        else:
            acc = acc + _nn(pe.astype(BF16), vn)
    o = jnp.where(own_head, acc / l, 0.0)
    out = o[0:tn, :]
    for h in range(1, N_HEADS):
        out = out + o[h * tn:(h + 1) * tn, :]
    o_ref[...] = out


def _sample_attn(layer, page_table_flat, q, kn, vn, lfn, cache_kt, cache_vt, cache_lft, *, npages):
    nseq, tn, _ = q.shape
    new = lambda width: pl.BlockSpec((None, tn, width), lambda b, pt: (b, 0, 0))

    def page(rows, j):
        return pl.BlockSpec((None, None, rows, PAGE), lambda b, pt: (layer, pt[b * npages + j], 0, 0))

    grid_spec = pltpu.PrefetchScalarGridSpec(
        num_scalar_prefetch=1,
        grid=(nseq,),
        in_specs=[new(ATTN_WIDTH), new(ATTN_WIDTH), new(ATTN_WIDTH),
                  pl.BlockSpec((None, N_HEADS, PAGE), lambda b, pt: (b, 0, 0))]
        + [page(ATTN_WIDTH, j) for j in range(npages)] * 2
        + [page(N_HEADS, j) for j in range(npages)],
        out_specs=new(ATTN_WIDTH),
        scratch_shapes=[pltpu.VMEM((N_HEADS * tn, (npages + 1) * PAGE), F32),
                        pltpu.VMEM(((npages + 1) * N_HEADS, PAGE), F32)],
    )
    return pl.pallas_call(
        functools.partial(_sample_attn_body, npages=npages),
        grid_spec=grid_spec,
        out_shape=jax.ShapeDtypeStruct((nseq, tn, ATTN_WIDTH), F32),
        compiler_params=pltpu.CompilerParams(dimension_semantics=("arbitrary",), vmem_limit_bytes=VMEM_LIMIT),
        name="sample_attn",
    )(page_table_flat, q, kn, vn, lfn, *([cache_kt] * npages), *([cache_vt] * npages), *([cache_lft] * npages))


FFN_CHUNK = 256


def _outproj_body(x_ref, za_ref, zb_ref, oc_ref, gate_ref, wa_ref, wb_ref, wc_ref, wo_ref, g2_ref,
                  wg_ref, wu_ref, wd_ref, gf_ref, y_ref, *, final):
    d = x_ref.shape[1]
    ya = _nn(za_ref[...].astype(BF16), wa_ref[...])
    yb = _nn(zb_ref[...].astype(BF16), wb_ref[...])
    yc = _nn(oc_ref[...].astype(BF16), wc_ref[...])
    merged = (gate_ref[:, 0:d].astype(F32) * ya + gate_ref[:, d:2 * d].astype(F32) * yb
              + gate_ref[:, 2 * d:3 * d].astype(F32) * yc)
    x1 = x_ref[...] + _nn(merged.astype(BF16), wo_ref[...])
    h2 = _rms(x1, g2_ref[...]).astype(BF16)
    acc = x1
    for c in range(0, wg_ref.shape[1], FFN_CHUNK):
        gch = _nn(h2, wg_ref[:, c:c + FFN_CHUNK])
        uch = _nn(h2, wu_ref[:, c:c + FFN_CHUNK])
        acc = acc + _nn((gch * _sigmoid(gch) * uch).astype(BF16), wd_ref[c:c + FFN_CHUNK, :])
    y_ref[...] = _rms(acc, gf_ref[...]) if final else acc


def _outproj(x2d, za, zb, oc, gates, w, gf, *, final, tm):
    n, d = x2d.shape
    assert n % tm == 0
    row = lambda a: pl.BlockSpec((tm, a.shape[1]), lambda i: (i, 0))
    weights = [w['wa'], w['wb'], w['wc'], w['wo'], w['norm2_g'], w['wg'], w['wu'], w['wd'], gf]
    return pl.pallas_call(
        functools.partial(_outproj_body, final=final),
        grid=(n // tm,),
        in_specs=[row(x2d), row(za), row(zb), row(oc), row(gates)] + [_const_spec(a.shape) for a in weights],
        out_specs=pl.BlockSpec((tm, d), lambda i: (i, 0)),
        out_shape=jax.ShapeDtypeStruct((n, d), F32),
        compiler_params=pltpu.CompilerParams(dimension_semantics=("arbitrary",), vmem_limit_bytes=VMEM_LIMIT),
        name="outproj_final" if final else "outproj",
    )(x2d, za, zb, oc, gates, *weights)


def _layer_weights(l, norm1_g, w_in, b_f, conv_a_w, conv_b_w, conv_b_bias, cf_norm_g, cf_norm_b,
                   w_a_out, w_b_out, w_c_out, w_o, norm2_g, w_ffn_gate, w_ffn_up, w_ffn_down):
    d = w_in.shape[1]
    wi = w_in[l]
    off_f = 3 * ATTN_WIDTH
    off_mix = off_f + N_HEADS
    off_gate = off_mix + 3 * SC_WIDTH + 2 * CF_WIDTH
    wq, wk, wv = (wi[:, i * ATTN_WIDTH:(i + 1) * ATTN_WIDTH].astype(BF16) for i in range(3))
    pad = LANES - N_HEADS
    return {
        'norm1_g': norm1_g[l].reshape(1, d),
        'wq': wq, 'wk': wk, 'wv': wv, 'wq_t': wq.T, 'wk_t': wk.T, 'wv_t': wv.T,
        'wf': jnp.pad(wi[:, off_f:off_mix], ((0, 0), (0, pad))).astype(BF16),
        'bf': jnp.pad(b_f[l], (0, pad)).reshape(1, LANES),
        'wmix': wi[:, off_mix:off_gate].astype(BF16),
        'wgate': wi[:, off_gate:].astype(BF16),
        'conv_a_w': conv_a_w[l], 'conv_b_w': conv_b_w[l], 'conv_b_bias': conv_b_bias[l].reshape(1, -1),
        'cf_norm_g': cf_norm_g[l].reshape(1, -1), 'cf_norm_b': cf_norm_b[l].reshape(1, -1),
        'wa': w_a_out[l].astype(BF16), 'wb': w_b_out[l].astype(BF16), 'wc': w_c_out[l].astype(BF16),
        'wo': w_o[l].astype(BF16), 'norm2_g': norm2_g[l].reshape(1, d),
        'wg': w_ffn_gate[l].astype(BF16), 'wu': w_ffn_up[l].astype(BF16), 'wd': w_ffn_down[l].astype(BF16),
    }


def kernel(x_prompt, x_sample, cache_k, cache_v, cache_logf, state_conv_a, state_conv_b, page_table, norm1_g, w_in, b_f, conv_a_w, conv_b_w, conv_b_bias, cf_norm_g, cf_norm_b, w_a_out, w_b_out, w_c_out, w_o, norm2_g, w_ffn_gate, w_ffn_up, w_ffn_down, final_norm_g):
    depth = w_in.shape[0]
    pb, ps, d = x_prompt.shape
    sb_, st, _ = x_sample.shape
    n_pool = cache_k.shape[1]
    npages = page_table.shape[1]
    assert cache_k.shape[2:] == (PAGE, N_HEADS, HEAD_DIM) and ps % PAGE == 0
    cache_kt = cache_k.transpose(0, 1, 3, 4, 2).reshape(depth, n_pool, ATTN_WIDTH, PAGE)
    cache_vt = cache_v.transpose(0, 1, 3, 4, 2).reshape(depth, n_pool, ATTN_WIDTH, PAGE)
    cache_lft = cache_logf.transpose(0, 1, 3, 2)
    pt_flat = page_table.reshape(-1)
    gf = final_norm_g.reshape(1, d)
    tm_p = 512 if (pb * ps) % 512 == 0 else PAGE
    tm_s = 512 if (sb_ * st) % 512 == 0 else sb_ * st
    tq = 256 if ps % 256 == 0 else PAGE
    bb = 16 if sb_ % 16 == 0 else sb_
    conv_rows = 64 if ps % 64 == 0 else ps

    xp = x_prompt.reshape(pb * ps, d)
    xs = x_sample.reshape(sb_ * st, d)
    zeros_a = jnp.zeros((pb, SC_KERNEL - 1, SC_WIDTH), F32)
    zeros_b = jnp.zeros((pb, CF_KERNEL - 1, CF_WIDTH), F32)
    outs = {k: [] for k in ('kp', 'vp', 'fp', 'ap', 'bp', 'ks', 'vs', 'fs', 'as', 'bs')}
    for l in range(depth):
        w = _layer_weights(l, norm1_g, w_in, b_f, conv_a_w, conv_b_w, conv_b_bias, cf_norm_g, cf_norm_b,
                           w_a_out, w_b_out, w_c_out, w_o, norm2_g, w_ffn_gate, w_ffn_up, w_ffn_down)
        final = l == depth - 1
        qt, kt, vt, kb, vtb, lft, sbp, ua, ub, gates = _inproj(xp, w, paged=True, tm=tm_p)
        za, zb, na, nb = _mixers(ua.reshape(pb, ps, -1), ub.reshape(pb, ps, -1), sbp.reshape(pb, ps, -1),
                                 zeros_a, zeros_b, w, bb=1, rows=conv_rows)
        oc = _prompt_attn(qt, kb, vtb, lft, batch=pb, tq=tq)
        xp = _outproj(xp, za.reshape(pb * ps, -1), zb.reshape(pb * ps, -1), oc, gates, w, gf, final=final, tm=tm_p)
        npp = ps // PAGE
        to_pages = lambda a: a.reshape(pb, npp, N_HEADS, HEAD_DIM, PAGE).transpose(0, 1, 4, 2, 3)
        outs['kp'].append(to_pages(kt))
        outs['vp'].append(to_pages(vt))
        outs['fp'].append(lft.reshape(pb, npp, N_HEADS, PAGE).transpose(0, 1, 3, 2))
        outs['ap'].append(na)
        outs['bp'].append(nb)
        q, k, v, _, _, lf, sbs, ua, ub, gates = _inproj(xs, w, paged=False, tm=tm_s)
        za, zb, na, nb = _mixers(ua.reshape(sb_, st, -1), ub.reshape(sb_, st, -1), sbs.reshape(sb_, st, -1),
                                 state_conv_a[l], state_conv_b[l], w, bb=bb, rows=st)
        lf = lf[:, :N_HEADS].reshape(sb_, st, N_HEADS)
        lfn = jnp.pad(lf.transpose(0, 2, 1), ((0, 0), (0, 0), (0, PAGE - st)))
        oc = _sample_attn(l, pt_flat, q.reshape(sb_, st, -1), k.reshape(sb_, st, -1), v.reshape(sb_, st, -1), lfn,
                          cache_kt, cache_vt, cache_lft, npages=npages)
        xs = _outproj(xs, za.reshape(sb_ * st, -1), zb.reshape(sb_ * st, -1), oc.reshape(sb_ * st, -1), gates, w, gf,
                      final=final, tm=tm_s)
        outs['ks'].append(k.reshape(sb_, st, N_HEADS, HEAD_DIM))
        outs['vs'].append(v.reshape(sb_, st, N_HEADS, HEAD_DIM))
        outs['fs'].append(lf)
        outs['as'].append(na)
        outs['bs'].append(nb)
    stack = lambda k: jnp.stack(outs[k])
    return (xp.reshape(pb, ps, d), xs.reshape(sb_, st, d), stack('kp'), stack('vp'), stack('fp'), stack('ap'),
            stack('bp'), stack('ks'), stack('vs'), stack('fs'), stack('as'), stack('bs'))
```

```python
import functools

import jax
import jax.numpy as jnp
from jax import lax
from jax.experimental import pallas as pl
from jax.experimental.pallas import tpu as pltpu

F32 = jnp.float32
BF16 = jnp.bfloat16

EPS = 1e-6
LOG2E = 1.4426950408889634
N_HEADS = 8
HEAD_DIM = 64
ATTN_WIDTH = N_HEADS * HEAD_DIM
PAGE = 128
SC_WIDTH = 256
SC_KERNEL = 3
CF_WIDTH = 256
CF_KERNEL = 31
N_BRANCHES = 3
LANES = 128
VMEM_LIMIT = 56 * 1024 * 1024


def _nt(a, b):
    return lax.dot_general(a, b, (((1,), (1,)), ((), ())), preferred_element_type=F32)


def _nn(a, b):
    return jnp.dot(a, b, preferred_element_type=F32)


def _split3(x):
    hi = x.astype(BF16)
    r1 = x - hi.astype(F32)
    mid = r1.astype(BF16)
    lo = (r1 - mid.astype(F32)).astype(BF16)
    return hi, mid, lo


def _exact_nn(x, w01):
    hi, mid, lo = _split3(x)
    return _nn(hi, w01) + _nn(mid, w01) + _nn(lo, w01)


def _sigmoid(x):
    return 1.0 / (1.0 + jnp.exp(-x))


def _log_sigmoid(x):
    return jnp.minimum(x, 0.0) - jnp.log(1.0 + jnp.exp(-jnp.abs(x)))


def _rms(x, g):
    return x * lax.rsqrt(jnp.mean(x * x, axis=-1, keepdims=True) + EPS) * g


def _const_spec(shape):
    nd = len(shape)
    return pl.BlockSpec(shape, lambda *_: (0,) * nd, pipeline_mode=pl.Buffered(1))


def _inproj_body(x_ref, g_ref, wq_ref, wk_ref, wv_ref, wf_ref, bf_ref, wmix_ref, wgate_ref, kprev_ref, vprev_ref,
                 lfprev_ref, q_ref, k_ref, v_ref, kb_ref, vb_ref, lf_ref, sb_ref, ua_ref, ub_ref, gate_ref, *, paged):
    h = _rms(x_ref[...], g_ref[...]).astype(BF16)
    tm = h.shape[0]
    last = k_ref.shape[0] - 1
    for j in range(last):
        k_ref[j] = kprev_ref[j]
        v_ref[j] = vprev_ref[j]
        if paged:
            lf_ref[j] = lfprev_ref[j]

    def mix(c0, c1):
        return _nn(h, wmix_ref[:, c0:c1])

    f = _nn(h, wf_ref[...]) + bf_ref[...]
    lf = _log_sigmoid(f)
    if paged:
        qt = (_nt(wq_ref[...], h) * (HEAD_DIM ** -0.5 * LOG2E)).astype(BF16)
        kt = _nt(wk_ref[...], h)
        vt = _nt(wv_ref[...], h)
        kb_ref[...] = kt.T.astype(BF16)
        lft = lf.T
        for p in range(tm // PAGE):
            sl = slice(p * PAGE, (p + 1) * PAGE)
            q_ref[p] = qt[:, sl]
            k_ref[last, p] = kt[:, sl]
            v_ref[last, p] = vt[:, sl]
            vb_ref[p] = vt[:, sl].astype(BF16)
            lf_ref[last, p] = lft[:N_HEADS, sl]
    else:
        q_ref[...] = _nn(h, wq_ref[...]) * (HEAD_DIM ** -0.5)
        k_ref[last] = _nn(h, wk_ref[...])
        v_ref[last] = _nn(h, wv_ref[...])
        lf_ref[...] = lf
    sb_ref[...] = mix(0, SC_WIDTH).astype(BF16)
    ua_ref[...] = mix(SC_WIDTH, 2 * SC_WIDTH) * mix(2 * SC_WIDTH, 3 * SC_WIDTH)
    c0 = 3 * SC_WIDTH
    ub_ref[...] = mix(c0, c0 + CF_WIDTH) * _sigmoid(mix(c0 + CF_WIDTH, c0 + 2 * CF_WIDTH))
    gw = 512
    for c in range(0, gate_ref.shape[1], gw):
        gate_ref[:, c:c + gw] = _sigmoid(_nn(h, wgate_ref[:, c:c + gw])).astype(BF16)


def _inproj(x2d, w, prev, *, paged, tm):
    n, d = x2d.shape
    assert n % tm == 0 and tm % PAGE == 0
    grid = (n // tm,)
    layers = 1 if prev is None else prev[0].shape[0] + 1
    row = lambda width, dt: (jax.ShapeDtypeStruct((n, width), dt), pl.BlockSpec((tm, width), lambda i: (i, 0)))
    row_l = lambda nl, width, dt: (jax.ShapeDtypeStruct((nl, n, width), dt),
                                   pl.BlockSpec((nl, tm, width), lambda i: (0, i, 0)))
    if paged:
        npg = tm // PAGE
        pg = lambda rows, dt: (jax.ShapeDtypeStruct((n // PAGE, rows, PAGE), dt),
                               pl.BlockSpec((npg, rows, PAGE), lambda i: (i, 0, 0)))
        pg_l = lambda nl, rows, dt: (jax.ShapeDtypeStruct((nl, n // PAGE, rows, PAGE), dt),
                                     pl.BlockSpec((nl, npg, rows, PAGE), lambda i: (0, i, 0, 0)))
        outs = [pg(ATTN_WIDTH, BF16), pg_l(layers, ATTN_WIDTH, F32), pg_l(layers, ATTN_WIDTH, F32),
                row(ATTN_WIDTH, BF16), pg(ATTN_WIDTH, BF16), pg_l(layers, N_HEADS, F32)]
        prev_specs = [pg_l(layers - 1, ATTN_WIDTH, F32), pg_l(layers - 1, ATTN_WIDTH, F32),
                      pg_l(layers - 1, N_HEADS, F32)]
        wq, wk, wv = w['wq_t'], w['wk_t'], w['wv_t']
    else:
        outs = [row(ATTN_WIDTH, F32), row_l(layers, ATTN_WIDTH, F32), row_l(layers, ATTN_WIDTH, F32), None, None,
                row(LANES, F32)]
        prev_specs = [row_l(layers - 1, ATTN_WIDTH, F32), row_l(layers - 1, ATTN_WIDTH, F32)]
        wq, wk, wv = w['wq'], w['wk'], w['wv']
    outs += [row(SC_WIDTH, BF16), row(SC_WIDTH, F32), row(CF_WIDTH, F32), row(N_BRANCHES * d, BF16)]
    live = [o for o in outs if o is not None]
    prev_in = [] if prev is None else list(prev[:len(prev_specs)])
    n_in = 9 + len(prev_in)

    def body(*refs):
        ins, out_refs = list(refs[:n_in]), list(refs[n_in:])
        prev_refs = ins[9:] + [None] * (3 - len(prev_in))
        full = [out_refs.pop(0) if o is not None else None for o in outs]
        _inproj_body(*ins[:9], *prev_refs, *full, paged=paged)

    weights = [w['norm1_g'], wq, wk, wv, w['wf'], w['bf'], w['wmix'], w['wgate']]
    res = pl.pallas_call(
        body,
        grid=grid,
        in_specs=[pl.BlockSpec((tm, d), lambda i: (i, 0))] + [_const_spec(a.shape) for a in weights]
        + [s[1] for s in prev_specs[:len(prev_in)]],
        out_specs=[o[1] for o in live],
        out_shape=[o[0] for o in live],
        compiler_params=pltpu.CompilerParams(dimension_semantics=("arbitrary",), vmem_limit_bytes=VMEM_LIMIT),
        name="inproj_paged" if paged else "inproj_rows",
    )(x2d, *weights, *prev_in)
    res = list(res)
    return [res.pop(0) if o is not None else None for o in outs]


SUBLANES = 8
A_PAD = 8
B_PAD = 32
A_WIN = 8
B_WIN = 40


def _shift_rows(x, first, n):
    if first % SUBLANES == 0:
        return x[:, first:first + n, :]
    total = x.shape[1]
    return pltpu.roll(x, total - first, 1)[:, :n, :]


def _mixers_body(ua_ref, ub_ref, sb_ref, ha_ref, hb_ref, wa_ref, wb_ref, bias_ref, lng_ref, lnb_ref,
                 za_ref, zb_ref, na_ref, nb_ref, sa, sbuf, *, rows):
    bb, t, _ = ua_ref.shape
    a0 = A_PAD - (SC_KERNEL - 1)
    b0 = B_PAD - (CF_KERNEL - 1)
    sa[:, a0:A_PAD, :] = ha_ref[...]
    sa[:, A_PAD:A_PAD + t, :] = ua_ref[...]
    sbuf[:, b0:B_PAD, :] = hb_ref[...]
    sbuf[:, B_PAD:B_PAD + t, :] = ub_ref[...]
    sbuf[:, B_PAD + t:, :] = jnp.zeros((bb, B_WIN - B_PAD, CF_WIDTH), F32)
    sa[:, 0:a0, :] = jnp.zeros((bb, a0, SC_WIDTH), F32)
    sbuf[:, 0:b0, :] = jnp.zeros((bb, b0, CF_WIDTH), F32)
    na_ref[...] = sa[:, t + a0:t + A_PAD, :]
    nb_ref[...] = sbuf[:, t + b0:t + B_PAD, :]
    bias, lng, lnb = bias_ref[...], lng_ref[...], lnb_ref[...]

    def convs(r0):
        halves = []
        for c in range(0, CF_WIDTH, LANES):
            win = sbuf[:, pl.ds(r0, rows + B_WIN), c:c + LANES]
            acc = None
            for b in range(SUBLANES):
                n_a = (CF_KERNEL - 1 - b) // SUBLANES + 1
                shifted = _shift_rows(win, b0 + b, rows + SUBLANES * (n_a - 1))
                for a in range(n_a):
                    i = SUBLANES * a + b
                    term = wb_ref[i:i + 1, c:c + LANES] * shifted[:, SUBLANES * a:SUBLANES * a + rows, :]
                    acc = term if acc is None else acc + term
            halves.append(acc)
        win_a = sa[:, pl.ds(r0, rows + A_WIN), :]
        acc_a = wa_ref[0:1, :] * _shift_rows(win_a, a0, rows)
        for i in range(1, SC_KERNEL):
            acc_a = acc_a + wa_ref[i:i + 1, :] * _shift_rows(win_a, a0 + i, rows)
        za_ref[:, pl.ds(r0, rows), :] = (sb_ref[:, pl.ds(r0, rows), :].astype(F32) * acc_a).astype(BF16)
        return jnp.concatenate(halves, axis=-1)

    def norm_act(z, r0):
        z = z + bias
        mu = jnp.mean(z, axis=-1, keepdims=True)
        zc = z - mu
        var = jnp.mean(zc * zc, axis=-1, keepdims=True)
        y = zc * lax.rsqrt(var + EPS) * lng + lnb
        zb_ref[:, pl.ds(r0, rows), :] = (y * _sigmoid(y)).astype(BF16)

    if t == rows:
        norm_act(convs(0), 0)
    else:
        def step(c, z_prev):
            r0 = pl.multiple_of(c * rows, rows)
            z = convs(r0)
            norm_act(z_prev, pl.multiple_of((c - 1) * rows, rows))
            return z
        z_last = lax.fori_loop(1, t // rows, step, convs(0))
        norm_act(z_last, t - rows)


def _mixers(ua, ub, sb, hist_a, hist_b, w, *, bb, rows):
    b, t, _ = ua.shape
    assert b % bb == 0 and t % rows == 0
    seq = lambda width: pl.BlockSpec((bb, t, width), lambda i: (i, 0, 0))
    hist = lambda r, width: pl.BlockSpec((bb, r, width), lambda i: (i, 0, 0))
    consts = [w['conv_a_w'], w['conv_b_w'], w['conv_b_bias'], w['cf_norm_g'], w['cf_norm_b']]
    return pl.pallas_call(
        functools.partial(_mixers_body, rows=rows),
        grid=(b // bb,),
        in_specs=[seq(SC_WIDTH), seq(CF_WIDTH), seq(SC_WIDTH), hist(SC_KERNEL - 1, SC_WIDTH),
                  hist(CF_KERNEL - 1, CF_WIDTH)] + [_const_spec(a.shape) for a in consts],
        out_specs=[seq(SC_WIDTH), seq(CF_WIDTH), hist(SC_KERNEL - 1, SC_WIDTH), hist(CF_KERNEL - 1, CF_WIDTH)],
        out_shape=[jax.ShapeDtypeStruct((b, t, SC_WIDTH), BF16), jax.ShapeDtypeStruct((b, t, CF_WIDTH), BF16),
                   jax.ShapeDtypeStruct((b, SC_KERNEL - 1, SC_WIDTH), F32),
                   jax.ShapeDtypeStruct((b, CF_KERNEL - 1, CF_WIDTH), F32)],
        scratch_shapes=[pltpu.VMEM((bb, t + A_WIN, SC_WIDTH), F32), pltpu.VMEM((bb, t + B_WIN, CF_WIDTH), F32)],
        compiler_params=pltpu.CompilerParams(dimension_semantics=("arbitrary",), vmem_limit_bytes=VMEM_LIMIT),
        name=f"mixers_t{t}",
    )(ua, ub, sb, hist_a, hist_b, *consts)


def _tri(n, fn):
    r = lax.broadcasted_iota(jnp.int32, (n, n), 0)
    c = lax.broadcasted_iota(jnp.int32, (n, n), 1)
    return jnp.where(fn(r, c), 1.0, 0.0).astype(BF16)


def _prompt_attn_body(qt_ref, k_ref, vt_ref, lft_ref, o_ref, ct_scr, ccol_scr, crep_scr, m_scr, l_scr, acc_scr,
                      *, tq, heads_per_step):
    npages = qt_ref.shape[0]
    t = npages * PAGE
    ppb = tq // PAGE
    x = lft_ref[...].reshape(npages * N_HEADS, PAGE)
    cw = _exact_nn(x, _tri(PAGE, lambda r, c: r <= c))
    run = jnp.zeros((N_HEADS, 1), F32)
    zpad = jnp.zeros((LANES - N_HEADS, PAGE), F32)
    for p in range(npages):
        cp = cw[p * N_HEADS:(p + 1) * N_HEADS, :] + run
        run = cp[:, PAGE - 1:PAGE]
        cp2 = cp * LOG2E
        ct_scr[p] = cp2
        ccol_scr[p * PAGE:(p + 1) * PAGE, :] = jnp.concatenate([cp2, zpad], axis=0).T
    zq = jnp.zeros((HEAD_DIM, tq), BF16)
    key_i = lax.broadcasted_iota(jnp.int32, (tq, tq), 0)
    qry_i = lax.broadcasted_iota(jnp.int32, (tq, tq), 1)
    for h in range(N_HEADS):
        for r in range(0, t, tq):
            crep_scr[h, r:r + tq, :] = jnp.broadcast_to(ccol_scr[r:r + tq, h:h + 1], (tq, LANES))

    def paged(ref, blk, rws):
        return jnp.concatenate([ref[blk * ppb + j, rws, :] for j in range(ppb)], axis=1)

    for h0 in range(0, N_HEADS, heads_per_step):
        heads = range(h0, h0 + heads_per_step)

        def q_block(qi, carry):
            q0 = pl.multiple_of(qi * tq, tq)
            for h in heads:
                m_scr[h] = jnp.full((1, tq), -jnp.inf, F32)
                l_scr[h] = jnp.zeros((1, tq), F32)
                acc_scr[h] = jnp.zeros((HEAD_DIM, tq), F32)

            def k_step(kj, masked):
                k0 = pl.multiple_of(kj * tq, tq)
                s, pe, alpha = {}, {}, {}

                def scores(h):
                    pair = slice((h // 2) * 2 * HEAD_DIM, (h // 2 + 1) * 2 * HEAD_DIM)
                    qh = paged(qt_ref, qi, slice(h * HEAD_DIM, (h + 1) * HEAD_DIM))
                    qtz = jnp.concatenate([qh, zq] if h % 2 == 0 else [zq, qh], axis=0)
                    cs = crep_scr[h, pl.ds(k0, tq), :]
                    sh = _nn(k_ref[pl.ds(k0, tq), pair], qtz) - jnp.concatenate([cs] * (tq // LANES), axis=1)
                    s[h] = jnp.where(key_i <= qry_i, sh, -jnp.inf) if masked else sh

                def softmax(h):
                    ct = jnp.concatenate([ct_scr[qi * ppb + j, h:h + 1, :] for j in range(ppb)], axis=1)
                    m = m_scr[h]
                    m_new = jnp.maximum(m, jnp.max(s[h], axis=0, keepdims=True) + ct)
                    alpha[h] = jnp.exp2(m - m_new)
                    p = jnp.exp2(s.pop(h) + (ct - m_new))
                    m_scr[h] = m_new
                    l_scr[h] = alpha[h] * l_scr[h] + jnp.sum(p, axis=0, keepdims=True)
                    pe[h] = p.astype(BF16)

                def weighted_values(h):
                    vt = paged(vt_ref, kj, slice(h * HEAD_DIM, (h + 1) * HEAD_DIM))
                    acc_scr[h] = acc_scr[h] * alpha.pop(h) + _nn(vt, pe.pop(h))

                hs = list(heads)
                for i in range(len(hs) + 2):
                    if i < len(hs):
                        scores(hs[i])
                    if 0 <= i - 2 < len(hs):
                        weighted_values(hs[i - 2])
                    if 0 <= i - 1 < len(hs):
                        softmax(hs[i - 1])

            def unmasked(kj, c):
                k_step(kj, False)
                return c

            lax.fori_loop(0, qi, unmasked, 0)
            k_step(qi, True)
            for hp in range(h0 // 2, (h0 + heads_per_step) // 2):
                ot = jnp.concatenate([acc_scr[2 * hp + e] / l_scr[2 * hp + e] for e in range(2)], axis=0)
                o_ref[pl.ds(q0, tq), hp * 2 * HEAD_DIM:(hp + 1) * 2 * HEAD_DIM] = ot.T.astype(o_ref.dtype)
            return carry

        lax.fori_loop(0, t // tq, q_block, 0)


ATTN_HEADS_PER_STEP = 8


def _prompt_attn(qt, kb, vtb, lft_layers, *, batch, tq):
    n, _ = kb.shape
    t = n // batch
    npages = t // PAGE
    layer = lft_layers.shape[0] - 1
    assert t % tq == 0 and tq % PAGE == 0
    paged_spec = pl.BlockSpec((npages, ATTN_WIDTH, PAGE), lambda b: (b, 0, 0))
    return pl.pallas_call(
        functools.partial(_prompt_attn_body, tq=tq, heads_per_step=ATTN_HEADS_PER_STEP),
        grid=(batch,),
        in_specs=[paged_spec, pl.BlockSpec((t, ATTN_WIDTH), lambda b: (b, 0)), paged_spec,
                  pl.BlockSpec((None, npages, N_HEADS, PAGE), lambda b: (layer, b, 0, 0))],
        out_specs=pl.BlockSpec((t, ATTN_WIDTH), lambda b: (b, 0)),
        out_shape=jax.ShapeDtypeStruct((n, ATTN_WIDTH), BF16),
        scratch_shapes=[pltpu.VMEM((npages, N_HEADS, PAGE), F32), pltpu.VMEM((t, LANES), F32),
                        pltpu.VMEM((N_HEADS, t, LANES), F32), pltpu.VMEM((N_HEADS, 1, tq), F32),
                        pltpu.VMEM((N_HEADS, 1, tq), F32), pltpu.VMEM((N_HEADS, HEAD_DIM, tq), F32)],
        compiler_params=pltpu.CompilerParams(dimension_semantics=("arbitrary",), vmem_limit_bytes=VMEM_LIMIT),
        name="prompt_attn",
    )(qt, kb, vtb, lft_layers)


def _sample_attn_body(pt_ref, q_ref, kn_ref, vn_ref, lfn_ref, *rest, npages):
    kt_refs = rest[:npages]
    vt_refs = rest[npages:2 * npages]
    lft_refs = rest[2 * npages:3 * npages]
    o_ref, s_scr, r_scr = rest[3 * npages:]
    tn = q_ref.shape[0]
    hq = N_HEADS * tn
    nblk = npages + 1

    row = lax.broadcasted_iota(jnp.int32, (hq, ATTN_WIDTH), 0)
    lane = lax.broadcasted_iota(jnp.int32, (hq, ATTN_WIDTH), 1)
    own_head = (lane // HEAD_DIM) == (row // tn)
    q = q_ref[...]
    qbd = jnp.where(own_head, jnp.concatenate([q] * N_HEADS, axis=0), 0.0).astype(BF16)

    x = jnp.concatenate([r[...] for r in lft_refs] + [lfn_ref[...]], axis=0)
    rw = _exact_nn(x, _tri(PAGE, lambda r, c: r > c))
    tot = rw[:, 0:1] + x[:, 0:1]
    run = jnp.zeros((N_HEADS, 1), F32)
    for p in reversed(range(nblk)):
        hs = slice(p * N_HEADS, (p + 1) * N_HEADS)
        r_scr[hs, :] = rw[hs, :] + run
        run = run + tot[hs, :]

    def bias_rows(p):
        return jnp.concatenate(
            [jnp.broadcast_to(r_scr[p * N_HEADS + h:p * N_HEADS + h + 1, :], (tn, PAGE)) for h in range(N_HEADS)],
            axis=0)

    row_p = lax.broadcasted_iota(jnp.int32, (hq, PAGE), 0) % tn
    lane_p = lax.broadcasted_iota(jnp.int32, (hq, PAGE), 1)
    b_new = bias_rows(npages)
    s_q = jnp.sum(jnp.where(lane_p == row_p, b_new, 0.0), axis=-1, keepdims=True)
    zrows = jnp.zeros((PAGE - tn, ATTN_WIDTH), F32)
    kn = jnp.concatenate([kn_ref[...], zrows], axis=0).astype(BF16)
    vn = jnp.concatenate([vn_ref[...], zrows], axis=0).astype(BF16)

    m = jnp.full((hq, 1), -jnp.inf, F32)
    for p in range(nblk):
        if p < npages:
            s = _nn(qbd, kt_refs[p][...].astype(BF16)) + bias_rows(p) - s_q
        else:
            s = jnp.where(lane_p <= row_p, _nt(qbd, kn) + b_new - s_q, -jnp.inf)
        s_scr[:, p * PAGE:(p + 1) * PAGE] = s
        m = jnp.maximum(m, jnp.max(s, axis=-1, keepdims=True))
    l = jnp.zeros((hq, 1), F32)
    acc = jnp.zeros((hq, ATTN_WIDTH), F32)
    for p in range(nblk):
        pe = jnp.exp(s_scr[:, p * PAGE:(p + 1) * PAGE] - m)
        l = l + jnp.sum(pe, axis=-1, keepdims=True)
        if p < npages:
            acc = acc + _nt(pe.astype(BF16), vt_refs[p][...].astype(BF16))
        else:
            acc = acc + _nn(pe.astype(BF16), vn)
    o = jnp.where(own_head, acc / l, 0.0)
    out = o[0:tn, :]
    for h in range(1, N_HEADS):
        out = out + o[h * tn:(h + 1) * tn, :]
    o_ref[...] = out


def _sample_attn(layer, page_table_flat, q, kn, vn, lfn, cache_kt, cache_vt, cache_lft, *, npages):
    nseq, tn, _ = q.shape
    new = lambda width: pl.BlockSpec((None, tn, width), lambda b, pt: (b, 0, 0))
    new_l = lambda width: pl.BlockSpec((None, None, tn, width), lambda b, pt: (layer, b, 0, 0))

    def page(rows, j):
        return pl.BlockSpec((None, None, rows, PAGE), lambda b, pt: (layer, pt[b * npages + j], 0, 0))

    grid_spec = pltpu.PrefetchScalarGridSpec(
        num_scalar_prefetch=1,
        grid=(nseq,),
        in_specs=[new(ATTN_WIDTH), new_l(ATTN_WIDTH), new_l(ATTN_WIDTH),
                  pl.BlockSpec((None, N_HEADS, PAGE), lambda b, pt: (b, 0, 0))]
        + [page(ATTN_WIDTH, j) for j in range(npages)] * 2
        + [page(N_HEADS, j) for j in range(npages)],
        out_specs=new(ATTN_WIDTH),
        scratch_shapes=[pltpu.VMEM((N_HEADS * tn, (npages + 1) * PAGE), F32),
                        pltpu.VMEM(((npages + 1) * N_HEADS, PAGE), F32)],
    )
    return pl.pallas_call(
        functools.partial(_sample_attn_body, npages=npages),
        grid_spec=grid_spec,
        out_shape=jax.ShapeDtypeStruct((nseq, tn, ATTN_WIDTH), F32),
        compiler_params=pltpu.CompilerParams(dimension_semantics=("arbitrary",), vmem_limit_bytes=VMEM_LIMIT),
        name="sample_attn",
    )(page_table_flat, q, kn, vn, lfn, *([cache_kt] * npages), *([cache_vt] * npages), *([cache_lft] * npages))


FFN_CHUNK = 256


def _outproj_body(x_ref, za_ref, zb_ref, oc_ref, gate_ref, wa_ref, wb_ref, wc_ref, wo_ref, g2_ref,
                  wg_ref, wu_ref, wd_ref, gf_ref, y_ref, *, final):
    d = x_ref.shape[1]
    ya = _nn(za_ref[...].astype(BF16), wa_ref[...])
    yb = _nn(zb_ref[...].astype(BF16), wb_ref[...])
    yc = _nn(oc_ref[...].astype(BF16), wc_ref[...])
    merged = (gate_ref[:, 0:d].astype(F32) * ya + gate_ref[:, d:2 * d].astype(F32) * yb
              + gate_ref[:, 2 * d:3 * d].astype(F32) * yc)
    x1 = x_ref[...] + _nn(merged.astype(BF16), wo_ref[...])
    h2 = _rms(x1, g2_ref[...]).astype(BF16)
    acc = x1
    for c in range(0, wg_ref.shape[1], FFN_CHUNK):
        gch = _nn(h2, wg_ref[:, c:c + FFN_CHUNK])
        uch = _nn(h2, wu_ref[:, c:c + FFN_CHUNK])
        acc = acc + _nn((gch * _sigmoid(gch) * uch).astype(BF16), wd_ref[c:c + FFN_CHUNK, :])
    y_ref[...] = _rms(acc, gf_ref[...]) if final else acc


def _outproj(x2d, za, zb, oc, gates, w, gf, *, final, tm):
    n, d = x2d.shape
    assert n % tm == 0
    row = lambda a: pl.BlockSpec((tm, a.shape[1]), lambda i: (i, 0))
    weights = [w['wa'], w['wb'], w['wc'], w['wo'], w['norm2_g'], w['wg'], w['wu'], w['wd'], gf]
    return pl.pallas_call(
        functools.partial(_outproj_body, final=final),
        grid=(n // tm,),
        in_specs=[row(x2d), row(za), row(zb), row(oc), row(gates)] + [_const_spec(a.shape) for a in weights],
        out_specs=pl.BlockSpec((tm, d), lambda i: (i, 0)),
        out_shape=jax.ShapeDtypeStruct((n, d), F32),
        compiler_params=pltpu.CompilerParams(dimension_semantics=("arbitrary",), vmem_limit_bytes=VMEM_LIMIT),
        name="outproj_final" if final else "outproj",
    )(x2d, za, zb, oc, gates, *weights)


def _layer_weights(l, norm1_g, w_in, b_f, conv_a_w, conv_b_w, conv_b_bias, cf_norm_g, cf_norm_b,
                   w_a_out, w_b_out, w_c_out, w_o, norm2_g, w_ffn_gate, w_ffn_up, w_ffn_down):
    d = w_in.shape[1]
    wi = w_in[l]
    off_f = 3 * ATTN_WIDTH
    off_mix = off_f + N_HEADS
    off_gate = off_mix + 3 * SC_WIDTH + 2 * CF_WIDTH
    wq, wk, wv = (wi[:, i * ATTN_WIDTH:(i + 1) * ATTN_WIDTH].astype(BF16) for i in range(3))
    pad = LANES - N_HEADS
    return {
        'norm1_g': norm1_g[l].reshape(1, d),
        'wq': wq, 'wk': wk, 'wv': wv, 'wq_t': wq.T, 'wk_t': wk.T, 'wv_t': wv.T,
        'wf': jnp.pad(wi[:, off_f:off_mix], ((0, 0), (0, pad))).astype(BF16),
        'bf': jnp.pad(b_f[l], (0, pad)).reshape(1, LANES),
        'wmix': wi[:, off_mix:off_gate].astype(BF16),
        'wgate': wi[:, off_gate:].astype(BF16),
        'conv_a_w': conv_a_w[l], 'conv_b_w': conv_b_w[l], 'conv_b_bias': conv_b_bias[l].reshape(1, -1),
        'cf_norm_g': cf_norm_g[l].reshape(1, -1), 'cf_norm_b': cf_norm_b[l].reshape(1, -1),
        'wa': w_a_out[l].astype(BF16), 'wb': w_b_out[l].astype(BF16), 'wc': w_c_out[l].astype(BF16),
        'wo': w_o[l].astype(BF16), 'norm2_g': norm2_g[l].reshape(1, d),
        'wg': w_ffn_gate[l].astype(BF16), 'wu': w_ffn_up[l].astype(BF16), 'wd': w_ffn_down[l].astype(BF16),
    }


def kernel(x_prompt, x_sample, cache_k, cache_v, cache_logf, state_conv_a, state_conv_b, page_table, norm1_g, w_in, b_f, conv_a_w, conv_b_w, conv_b_bias, cf_norm_g, cf_norm_b, w_a_out, w_b_out, w_c_out, w_o, norm2_g, w_ffn_gate, w_ffn_up, w_ffn_down, final_norm_g):
    depth = w_in.shape[0]
    pb, ps, d = x_prompt.shape
    sb_, st, _ = x_sample.shape
    n_pool = cache_k.shape[1]
    npages = page_table.shape[1]
    assert cache_k.shape[2:] == (PAGE, N_HEADS, HEAD_DIM) and ps % PAGE == 0
    cache_kt = cache_k.transpose(0, 1, 3, 4, 2).reshape(depth, n_pool, ATTN_WIDTH, PAGE)
    cache_vt = cache_v.transpose(0, 1, 3, 4, 2).reshape(depth, n_pool, ATTN_WIDTH, PAGE)
    cache_lft = cache_logf.transpose(0, 1, 3, 2)
    pt_flat = page_table.reshape(-1)
    gf = final_norm_g.reshape(1, d)
    tm_p = 512 if (pb * ps) % 512 == 0 else PAGE
    tm_s = 512 if (sb_ * st) % 512 == 0 else sb_ * st
    tq = 256 if ps % 256 == 0 else PAGE
    bb = 16 if sb_ % 16 == 0 else sb_
    conv_rows = 64 if ps % 64 == 0 else ps

    xp = x_prompt.reshape(pb * ps, d)
    xs = x_sample.reshape(sb_ * st, d)
    zeros_a = jnp.zeros((pb, SC_KERNEL - 1, SC_WIDTH), F32)
    zeros_b = jnp.zeros((pb, CF_KERNEL - 1, CF_WIDTH), F32)
    outs = {k: [] for k in ('ap', 'bp', 'fs', 'as', 'bs')}
    prev_p = prev_s = None
    npp = ps // PAGE
    for l in range(depth):
        w = _layer_weights(l, norm1_g, w_in, b_f, conv_a_w, conv_b_w, conv_b_bias, cf_norm_g, cf_norm_b,
                           w_a_out, w_b_out, w_c_out, w_o, norm2_g, w_ffn_gate, w_ffn_up, w_ffn_down)
        final = l == depth - 1
        qt, kt, vt, kb, vtb, lft, sbp, ua, ub, gates = _inproj(xp, w, prev_p, paged=True, tm=tm_p)
        prev_p = (kt, vt, lft)
        za, zb, na, nb = _mixers(ua.reshape(pb, ps, -1), ub.reshape(pb, ps, -1), sbp.reshape(pb, ps, -1),
                                 zeros_a, zeros_b, w, bb=1, rows=conv_rows)
        oc = _prompt_attn(qt, kb, vtb, lft, batch=pb, tq=tq)
        xp = _outproj(xp, za.reshape(pb * ps, -1), zb.reshape(pb * ps, -1), oc, gates, w, gf, final=final, tm=tm_p)
        outs['ap'].append(na)
        outs['bp'].append(nb)
        q, k, v, _, _, lf, sbs, ua, ub, gates = _inproj(xs, w, prev_s, paged=False, tm=tm_s)
        prev_s = (k, v)
        za, zb, na, nb = _mixers(ua.reshape(sb_, st, -1), ub.reshape(sb_, st, -1), sbs.reshape(sb_, st, -1),
                                 state_conv_a[l], state_conv_b[l], w, bb=bb, rows=st)
        lf = lf[:, :N_HEADS].reshape(sb_, st, N_HEADS)
        lfn = jnp.pad(lf.transpose(0, 2, 1), ((0, 0), (0, 0), (0, PAGE - st)))
        oc = _sample_attn(l, pt_flat, q.reshape(sb_, st, -1), k.reshape(l + 1, sb_, st, -1),
                          v.reshape(l + 1, sb_, st, -1), lfn, cache_kt, cache_vt, cache_lft, npages=npages)
        xs = _outproj(xs, za.reshape(sb_ * st, -1), zb.reshape(sb_ * st, -1), oc.reshape(sb_ * st, -1), gates, w, gf,
                      final=final, tm=tm_s)
        outs['fs'].append(lf)
        outs['as'].append(na)
        outs['bs'].append(nb)
    stack = lambda k: jnp.stack(outs[k])
    kt, vt, lft = prev_p
    to_pages = lambda a: a.reshape(depth, pb, npp, N_HEADS, HEAD_DIM, PAGE).transpose(0, 1, 2, 5, 3, 4)
    ks, vs = (a.reshape(depth, sb_, st, N_HEADS, HEAD_DIM) for a in prev_s)
    return (xp.reshape(pb, ps, d), xs.reshape(sb_, st, d), to_pages(kt), to_pages(vt),
            lft.reshape(depth, pb, npp, N_HEADS, PAGE).transpose(0, 1, 2, 4, 3), stack('ap'), stack('bp'),
            ks, vs, stack('fs'), stack('as'), stack('bs'))
```

```python
import functools

import jax
import jax.numpy as jnp
from jax import lax
from jax.experimental import pallas as pl
from jax.experimental.pallas import tpu as pltpu

F32 = jnp.float32
BF16 = jnp.bfloat16

EPS = 1e-6
LOG2E = 1.4426950408889634
N_HEADS = 8
HEAD_DIM = 64
ATTN_WIDTH = N_HEADS * HEAD_DIM
PAGE = 128
SC_WIDTH = 256
SC_KERNEL = 3
CF_WIDTH = 256
CF_KERNEL = 31
N_BRANCHES = 3
LANES = 128
VMEM_LIMIT = 56 * 1024 * 1024


def _nt(a, b):
    return lax.dot_general(a, b, (((1,), (1,)), ((), ())), preferred_element_type=F32)


def _nn(a, b):
    return jnp.dot(a, b, preferred_element_type=F32)


def _split3(x):
    hi = x.astype(BF16)
    r1 = x - hi.astype(F32)
    mid = r1.astype(BF16)
    lo = (r1 - mid.astype(F32)).astype(BF16)
    return hi, mid, lo


def _exact_nn(x, w01):
    hi, mid, lo = _split3(x)
    return _nn(hi, w01) + _nn(mid, w01) + _nn(lo, w01)


def _sigmoid(x):
    return 1.0 / (1.0 + jnp.exp(-x))


def _log_sigmoid(x):
    return jnp.minimum(x, 0.0) - jnp.log(1.0 + jnp.exp(-jnp.abs(x)))


def _rms(x, g):
    return x * lax.rsqrt(jnp.mean(x * x, axis=-1, keepdims=True) + EPS) * g


def _const_spec(shape):
    nd = len(shape)
    return pl.BlockSpec(shape, lambda *_: (0,) * nd, pipeline_mode=pl.Buffered(1))


def _inproj_body(x_ref, g_ref, wq_ref, wk_ref, wv_ref, wf_ref, bf_ref, wmix_ref, wgate_ref, kprev_ref, vprev_ref,
                 lfprev_ref, q_ref, k_ref, v_ref, kb_ref, vb_ref, lf_ref, sb_ref, ua_ref, ub_ref, gate_ref, *, paged):
    h = _rms(x_ref[...], g_ref[...]).astype(BF16)
    tm = h.shape[0]
    last = k_ref.shape[0] - 1
    for j in range(last):
        k_ref[j] = kprev_ref[j]
        v_ref[j] = vprev_ref[j]
        if paged:
            lf_ref[j] = lfprev_ref[j]

    def mix(c0, c1):
        return _nn(h, wmix_ref[:, c0:c1])

    f = _nn(h, wf_ref[...]) + bf_ref[...]
    lf = _log_sigmoid(f)
    if paged:
        qt = (_nt(wq_ref[...], h) * (HEAD_DIM ** -0.5 * LOG2E)).astype(BF16)
        kt = _nt(wk_ref[...], h)
        vt = _nt(wv_ref[...], h)
        kb_ref[...] = kt.T.astype(BF16)
        lft = lf.T
        for p in range(tm // PAGE):
            sl = slice(p * PAGE, (p + 1) * PAGE)
            q_ref[p] = qt[:, sl]
            k_ref[last, p] = kt[:, sl]
            v_ref[last, p] = vt[:, sl]
            vb_ref[p] = vt[:, sl].astype(BF16)
            lf_ref[last, p] = lft[:N_HEADS, sl]
    else:
        q_ref[...] = _nn(h, wq_ref[...]) * (HEAD_DIM ** -0.5)
        k_ref[last] = _nn(h, wk_ref[...])
        v_ref[last] = _nn(h, wv_ref[...])
        lf_ref[...] = lf
    sb_ref[...] = mix(0, SC_WIDTH).astype(BF16)
    ua_ref[...] = mix(SC_WIDTH, 2 * SC_WIDTH) * mix(2 * SC_WIDTH, 3 * SC_WIDTH)
    c0 = 3 * SC_WIDTH
    ub_ref[...] = mix(c0, c0 + CF_WIDTH) * _sigmoid(mix(c0 + CF_WIDTH, c0 + 2 * CF_WIDTH))
    gw = 512
    for c in range(0, gate_ref.shape[1], gw):
        gate_ref[:, c:c + gw] = _sigmoid(_nn(h, wgate_ref[:, c:c + gw])).astype(BF16)


def _inproj(x2d, w, prev, *, paged, tm):
    n, d = x2d.shape
    assert n % tm == 0 and tm % PAGE == 0
    grid = (n // tm,)
    layers = 1 if prev is None else prev[0].shape[0] + 1
    row = lambda width, dt: (jax.ShapeDtypeStruct((n, width), dt), pl.BlockSpec((tm, width), lambda i: (i, 0)))
    row_l = lambda nl, width, dt: (jax.ShapeDtypeStruct((nl, n, width), dt),
                                   pl.BlockSpec((nl, tm, width), lambda i: (0, i, 0)))
    if paged:
        npg = tm // PAGE
        pg = lambda rows, dt: (jax.ShapeDtypeStruct((n // PAGE, rows, PAGE), dt),
                               pl.BlockSpec((npg, rows, PAGE), lambda i: (i, 0, 0)))
        pg_l = lambda nl, rows, dt: (jax.ShapeDtypeStruct((nl, n // PAGE, rows, PAGE), dt),
                                     pl.BlockSpec((nl, npg, rows, PAGE), lambda i: (0, i, 0, 0)))
        outs = [pg(ATTN_WIDTH, BF16), pg_l(layers, ATTN_WIDTH, F32), pg_l(layers, ATTN_WIDTH, F32),
                row(ATTN_WIDTH, BF16), pg(ATTN_WIDTH, BF16), pg_l(layers, N_HEADS, F32)]
        prev_specs = [pg_l(layers - 1, ATTN_WIDTH, F32), pg_l(layers - 1, ATTN_WIDTH, F32),
                      pg_l(layers - 1, N_HEADS, F32)]
        wq, wk, wv = w['wq_t'], w['wk_t'], w['wv_t']
    else:
        outs = [row(ATTN_WIDTH, F32), row_l(layers, ATTN_WIDTH, F32), row_l(layers, ATTN_WIDTH, F32), None, None,
                row(LANES, F32)]
        prev_specs = [row_l(layers - 1, ATTN_WIDTH, F32), row_l(layers - 1, ATTN_WIDTH, F32)]
        wq, wk, wv = w['wq'], w['wk'], w['wv']
    outs += [row(SC_WIDTH, BF16), row(SC_WIDTH, F32), row(CF_WIDTH, F32), row(N_BRANCHES * d, BF16)]
    live = [o for o in outs if o is not None]
    prev_in = [] if prev is None else list(prev[:len(prev_specs)])
    n_in = 9 + len(prev_in)

    def body(*refs):
        ins, out_refs = list(refs[:n_in]), list(refs[n_in:])
        prev_refs = ins[9:] + [None] * (3 - len(prev_in))
        full = [out_refs.pop(0) if o is not None else None for o in outs]
        _inproj_body(*ins[:9], *prev_refs, *full, paged=paged)

    weights = [w['norm1_g'], wq, wk, wv, w['wf'], w['bf'], w['wmix'], w['wgate']]
    res = pl.pallas_call(
        body,
        grid=grid,
        in_specs=[pl.BlockSpec((tm, d), lambda i: (i, 0))] + [_const_spec(a.shape) for a in weights]
        + [s[1] for s in prev_specs[:len(prev_in)]],
        out_specs=[o[1] for o in live],
        out_shape=[o[0] for o in live],
        compiler_params=pltpu.CompilerParams(dimension_semantics=("arbitrary",), vmem_limit_bytes=VMEM_LIMIT),
        name="inproj_paged" if paged else "inproj_rows",
    )(x2d, *weights, *prev_in)
    res = list(res)
    return [res.pop(0) if o is not None else None for o in outs]


SUBLANES = 8
A_PAD = 8
B_PAD = 32
A_WIN = 8
B_WIN = 40


def _shift_rows(x, first, n):
    if first % SUBLANES == 0:
        return x[:, first:first + n, :]
    total = x.shape[1]
    return pltpu.roll(x, total - first, 1)[:, :n, :]


def _mixers_body(ua_ref, ub_ref, sb_ref, ha_ref, hb_ref, wa_ref, wb_ref, bias_ref, lng_ref, lnb_ref,
                 za_ref, zb_ref, na_ref, nb_ref, sa, sbuf, *, rows):
    bb, t, _ = ua_ref.shape
    a0 = A_PAD - (SC_KERNEL - 1)
    b0 = B_PAD - (CF_KERNEL - 1)
    sa[:, a0:A_PAD, :] = ha_ref[...]
    sa[:, A_PAD:A_PAD + t, :] = ua_ref[...]
    sbuf[:, b0:B_PAD, :] = hb_ref[...]
    sbuf[:, B_PAD:B_PAD + t, :] = ub_ref[...]
    sbuf[:, B_PAD + t:, :] = jnp.zeros((bb, B_WIN - B_PAD, CF_WIDTH), F32)
    sa[:, 0:a0, :] = jnp.zeros((bb, a0, SC_WIDTH), F32)
    sbuf[:, 0:b0, :] = jnp.zeros((bb, b0, CF_WIDTH), F32)
    na_ref[...] = sa[:, t + a0:t + A_PAD, :]
    nb_ref[...] = sbuf[:, t + b0:t + B_PAD, :]
    bias, lng, lnb = bias_ref[...], lng_ref[...], lnb_ref[...]

    def convs(r0):
        halves = []
        for c in range(0, CF_WIDTH, LANES):
            win = sbuf[:, pl.ds(r0, rows + B_WIN), c:c + LANES]
            acc = None
            for b in range(SUBLANES):
                n_a = (CF_KERNEL - 1 - b) // SUBLANES + 1
                shifted = _shift_rows(win, b0 + b, rows + SUBLANES * (n_a - 1))
                for a in range(n_a):
                    i = SUBLANES * a + b
                    term = wb_ref[i:i + 1, c:c + LANES] * shifted[:, SUBLANES * a:SUBLANES * a + rows, :]
                    acc = term if acc is None else acc + term
            halves.append(acc)
        win_a = sa[:, pl.ds(r0, rows + A_WIN), :]
        acc_a = wa_ref[0:1, :] * _shift_rows(win_a, a0, rows)
        for i in range(1, SC_KERNEL):
            acc_a = acc_a + wa_ref[i:i + 1, :] * _shift_rows(win_a, a0 + i, rows)
        za_ref[:, pl.ds(r0, rows), :] = (sb_ref[:, pl.ds(r0, rows), :].astype(F32) * acc_a).astype(BF16)
        return jnp.concatenate(halves, axis=-1)

    def norm_act(z, r0):
        z = z + bias
        mu = jnp.mean(z, axis=-1, keepdims=True)
        zc = z - mu
        var = jnp.mean(zc * zc, axis=-1, keepdims=True)
        y = zc * lax.rsqrt(var + EPS) * lng + lnb
        zb_ref[:, pl.ds(r0, rows), :] = (y * _sigmoid(y)).astype(BF16)

    if t == rows:
        norm_act(convs(0), 0)
    else:
        def step(c, z_prev):
            r0 = pl.multiple_of(c * rows, rows)
            z = convs(r0)
            norm_act(z_prev, pl.multiple_of((c - 1) * rows, rows))
            return z
        z_last = lax.fori_loop(1, t // rows, step, convs(0))
        norm_act(z_last, t - rows)


def _mixers(ua, ub, sb, hist_a, hist_b, w, *, bb, rows):
    b, t, _ = ua.shape
    assert b % bb == 0 and t % rows == 0
    seq = lambda width: pl.BlockSpec((bb, t, width), lambda i: (i, 0, 0))
    hist = lambda r, width: pl.BlockSpec((bb, r, width), lambda i: (i, 0, 0))
    consts = [w['conv_a_w'], w['conv_b_w'], w['conv_b_bias'], w['cf_norm_g'], w['cf_norm_b']]
    return pl.pallas_call(
        functools.partial(_mixers_body, rows=rows),
        grid=(b // bb,),
        in_specs=[seq(SC_WIDTH), seq(CF_WIDTH), seq(SC_WIDTH), hist(SC_KERNEL - 1, SC_WIDTH),
                  hist(CF_KERNEL - 1, CF_WIDTH)] + [_const_spec(a.shape) for a in consts],
        out_specs=[seq(SC_WIDTH), seq(CF_WIDTH), hist(SC_KERNEL - 1, SC_WIDTH), hist(CF_KERNEL - 1, CF_WIDTH)],
        out_shape=[jax.ShapeDtypeStruct((b, t, SC_WIDTH), BF16), jax.ShapeDtypeStruct((b, t, CF_WIDTH), BF16),
                   jax.ShapeDtypeStruct((b, SC_KERNEL - 1, SC_WIDTH), F32),
                   jax.ShapeDtypeStruct((b, CF_KERNEL - 1, CF_WIDTH), F32)],
        scratch_shapes=[pltpu.VMEM((bb, t + A_WIN, SC_WIDTH), F32), pltpu.VMEM((bb, t + B_WIN, CF_WIDTH), F32)],
        compiler_params=pltpu.CompilerParams(dimension_semantics=("arbitrary",), vmem_limit_bytes=VMEM_LIMIT),
        name=f"mixers_t{t}",
    )(ua, ub, sb, hist_a, hist_b, *consts)


def _tri(n, fn):
    r = lax.broadcasted_iota(jnp.int32, (n, n), 0)
    c = lax.broadcasted_iota(jnp.int32, (n, n), 1)
    return jnp.where(fn(r, c), 1.0, 0.0).astype(BF16)


def _prompt_attn_body(qt_ref, k_ref, vt_ref, lft_ref, o_ref, ct_scr, ccol_scr, crep_scr, m_scr, l_scr, acc_scr,
                      *, tq, heads_per_step):
    npages = qt_ref.shape[0]
    t = npages * PAGE
    ppb = tq // PAGE
    x = lft_ref[...].reshape(npages * N_HEADS, PAGE)
    cw = _exact_nn(x, _tri(PAGE, lambda r, c: r <= c))
    run = jnp.zeros((N_HEADS, 1), F32)
    zpad = jnp.zeros((LANES - N_HEADS, PAGE), F32)
    for p in range(npages):
        cp = cw[p * N_HEADS:(p + 1) * N_HEADS, :] + run
        run = cp[:, PAGE - 1:PAGE]
        cp2 = cp * LOG2E
        ct_scr[p] = cp2
        ccol_scr[p * PAGE:(p + 1) * PAGE, :] = jnp.concatenate([cp2, zpad], axis=0).T
    zq = jnp.zeros((HEAD_DIM, tq), BF16)
    key_i = lax.broadcasted_iota(jnp.int32, (tq, tq), 0)
    qry_i = lax.broadcasted_iota(jnp.int32, (tq, tq), 1)
    for h in range(N_HEADS):
        for r in range(0, t, tq):
            crep_scr[h, r:r + tq, :] = jnp.broadcast_to(ccol_scr[r:r + tq, h:h + 1], (tq, LANES))

    def paged(ref, blk, rws):
        return jnp.concatenate([ref[blk * ppb + j, rws, :] for j in range(ppb)], axis=1)

    for h0 in range(0, N_HEADS, heads_per_step):
        heads = range(h0, h0 + heads_per_step)

        def q_block(qi, carry):
            q0 = pl.multiple_of(qi * tq, tq)
            for h in heads:
                m_scr[h] = jnp.full((1, tq), -jnp.inf, F32)
                l_scr[h] = jnp.zeros((1, tq), F32)
                acc_scr[h] = jnp.zeros((HEAD_DIM, tq), F32)

            def k_step(kj, masked):
                k0 = pl.multiple_of(kj * tq, tq)
                s, pe, alpha = {}, {}, {}

                def scores(h):
                    pair = slice((h // 2) * 2 * HEAD_DIM, (h // 2 + 1) * 2 * HEAD_DIM)
                    qh = paged(qt_ref, qi, slice(h * HEAD_DIM, (h + 1) * HEAD_DIM))
                    qtz = jnp.concatenate([qh, zq] if h % 2 == 0 else [zq, qh], axis=0)
                    cs = crep_scr[h, pl.ds(k0, tq), :]
                    sh = _nn(k_ref[pl.ds(k0, tq), pair], qtz) - jnp.concatenate([cs] * (tq // LANES), axis=1)
                    s[h] = jnp.where(key_i <= qry_i, sh, -jnp.inf) if masked else sh

                def softmax(h):
                    ct = jnp.concatenate([ct_scr[qi * ppb + j, h:h + 1, :] for j in range(ppb)], axis=1)
                    m = m_scr[h]
                    m_new = jnp.maximum(m, jnp.max(s[h], axis=0, keepdims=True) + ct)
                    alpha[h] = jnp.exp2(m - m_new)
                    p = jnp.exp2(s.pop(h) + (ct - m_new))
                    m_scr[h] = m_new
                    l_scr[h] = alpha[h] * l_scr[h] + jnp.sum(p, axis=0, keepdims=True)
                    pe[h] = p.astype(BF16)

                def weighted_values(h):
                    vt = paged(vt_ref, kj, slice(h * HEAD_DIM, (h + 1) * HEAD_DIM))
                    acc_scr[h] = acc_scr[h] * alpha.pop(h) + _nn(vt, pe.pop(h))

                hs = list(heads)
                for i in range(len(hs) + 2):
                    if i < len(hs):
                        scores(hs[i])
                    if 0 <= i - 2 < len(hs):
                        weighted_values(hs[i - 2])
                    if 0 <= i - 1 < len(hs):
                        softmax(hs[i - 1])

            def unmasked(kj, c):
                k_step(kj, False)
                return c

            lax.fori_loop(0, qi, unmasked, 0)
            k_step(qi, True)
            for hp in range(h0 // 2, (h0 + heads_per_step) // 2):
                ot = jnp.concatenate([acc_scr[2 * hp + e] / l_scr[2 * hp + e] for e in range(2)], axis=0)
                o_ref[pl.ds(q0, tq), hp * 2 * HEAD_DIM:(hp + 1) * 2 * HEAD_DIM] = ot.T.astype(o_ref.dtype)
            return carry

        lax.fori_loop(0, t // tq, q_block, 0)


ATTN_HEADS_PER_STEP = 8


def _prompt_attn(qt, kb, vtb, lft_layers, *, batch, tq):
    n, _ = kb.shape
    t = n // batch
    npages = t // PAGE
    layer = lft_layers.shape[0] - 1
    assert t % tq == 0 and tq % PAGE == 0
    paged_spec = pl.BlockSpec((npages, ATTN_WIDTH, PAGE), lambda b: (b, 0, 0))
    return pl.pallas_call(
        functools.partial(_prompt_attn_body, tq=tq, heads_per_step=ATTN_HEADS_PER_STEP),
        grid=(batch,),
        in_specs=[paged_spec, pl.BlockSpec((t, ATTN_WIDTH), lambda b: (b, 0)), paged_spec,
                  pl.BlockSpec((None, npages, N_HEADS, PAGE), lambda b: (layer, b, 0, 0))],
        out_specs=pl.BlockSpec((t, ATTN_WIDTH), lambda b: (b, 0)),
        out_shape=jax.ShapeDtypeStruct((n, ATTN_WIDTH), BF16),
        scratch_shapes=[pltpu.VMEM((npages, N_HEADS, PAGE), F32), pltpu.VMEM((t, LANES), F32),
                        pltpu.VMEM((N_HEADS, t, LANES), F32), pltpu.VMEM((N_HEADS, 1, tq), F32),
                        pltpu.VMEM((N_HEADS, 1, tq), F32), pltpu.VMEM((N_HEADS, HEAD_DIM, tq), F32)],
        compiler_params=pltpu.CompilerParams(dimension_semantics=("arbitrary",), vmem_limit_bytes=VMEM_LIMIT),
        name="prompt_attn",
    )(qt, kb, vtb, lft_layers)


def _bias_rows(r_scr, blk, tn):
    return jnp.concatenate(
        [jnp.broadcast_to(r_scr[blk * N_HEADS + h:blk * N_HEADS + h + 1, :], (tn, PAGE)) for h in range(N_HEADS)],
        axis=0)


def _sample_setup(q, lft_ref, lfn, r_scr, npages):
    tn = q.shape[0]
    hq = N_HEADS * tn
    row = lax.broadcasted_iota(jnp.int32, (hq, ATTN_WIDTH), 0)
    lane = lax.broadcasted_iota(jnp.int32, (hq, ATTN_WIDTH), 1)
    own_head = (lane // HEAD_DIM) == (row // tn)
    qbd = jnp.where(own_head, jnp.concatenate([q] * N_HEADS, axis=0), 0.0).astype(BF16)
    nblk = npages + 1
    x = jnp.concatenate([lft_ref[p] for p in range(npages)] + [lfn], axis=0)
    rw = _exact_nn(x, _tri(PAGE, lambda r, c: r > c))
    tot = rw[:, 0:1] + x[:, 0:1]
    run = jnp.zeros((N_HEADS, 1), F32)
    for p in reversed(range(nblk)):
        hs = slice(p * N_HEADS, (p + 1) * N_HEADS)
        r_scr[hs, :] = rw[hs, :] + run
        run = run + tot[hs, :]
    row_p = lax.broadcasted_iota(jnp.int32, (hq, PAGE), 0) % tn
    lane_p = lax.broadcasted_iota(jnp.int32, (hq, PAGE), 1)
    b_new = _bias_rows(r_scr, npages, tn)
    s_q = jnp.sum(jnp.where(lane_p == row_p, b_new, 0.0), axis=-1, keepdims=True)
    return qbd, own_head, b_new, s_q, lane_p <= row_p


def _sample_scores(qbd, s_q, k_pages, page0, r_scr, s_scr, m_prev, new):
    hq = qbd.shape[0]
    tn = hq // N_HEADS
    hp = k_pages.shape[0]
    m = m_prev
    for pp in range(0, hp, 2):
        kt2 = jnp.concatenate([k_pages[pp], k_pages[pp + 1]], axis=1).astype(BF16)
        bias = jnp.concatenate([_bias_rows(r_scr, page0 + pp, tn), _bias_rows(r_scr, page0 + pp + 1, tn)], axis=1)
        s = _nn(qbd, kt2) + bias - s_q
        s_scr[:, pp * PAGE:(pp + 2) * PAGE] = s
        m = jnp.maximum(m, jnp.max(s, axis=-1, keepdims=True))
    if new is not None:
        kn, vn, b_new, causal = new
        s = jnp.where(causal, _nt(qbd, kn) + b_new - s_q, -jnp.inf)
        s_scr[:, hp * PAGE:(hp + 1) * PAGE] = s
        m = jnp.maximum(m, jnp.max(s, axis=-1, keepdims=True))
    return m


def _sample_accumulate(v_pages, s_scr, state, m, new):
    m_prev, l_prev, acc_prev = state
    hp = v_pages.shape[0]
    alpha = jnp.exp(m_prev - m)
    l = alpha * l_prev
    acc = alpha * acc_prev
    for pp in range(0, hp, 2):
        pe = jnp.exp(s_scr[:, pp * PAGE:(pp + 2) * PAGE] - m)
        l = l + jnp.sum(pe, axis=-1, keepdims=True)
        vt2 = jnp.concatenate([v_pages[pp], v_pages[pp + 1]], axis=1).astype(BF16)
        acc = acc + _nt(pe.astype(BF16), vt2)
    if new is not None:
        vn = new[1]
        pe = jnp.exp(s_scr[:, hp * PAGE:(hp + 1) * PAGE] - m)
        l = l + jnp.sum(pe, axis=-1, keepdims=True)
        acc = acc + _nn(pe.astype(BF16), vn)
    return m, l, acc


FFN_CHUNK = 256
RING_SLOTS = 2


def _outproj_math(x_ref, za_ref, zb_ref, oc_ref, gate_ref, wa_ref, wb_ref, wc_ref, wo_ref, g2_ref,
                  wg_ref, wu_ref, wd_ref, gf_ref, y_ref, *, final, before_chunk=None, after_chunk=None):
    d = x_ref.shape[1]
    ya = _nn(za_ref[...].astype(BF16), wa_ref[...])
    yb = _nn(zb_ref[...].astype(BF16), wb_ref[...])
    yc = _nn(oc_ref[...].astype(BF16), wc_ref[...])
    merged = (gate_ref[:, 0:d].astype(F32) * ya + gate_ref[:, d:2 * d].astype(F32) * yb
              + gate_ref[:, 2 * d:3 * d].astype(F32) * yc)
    x1 = x_ref[...] + _nn(merged.astype(BF16), wo_ref[...])
    h2 = _rms(x1, g2_ref[...]).astype(BF16)
    acc = x1
    for k, c in enumerate(range(0, wg_ref.shape[1], FFN_CHUNK)):
        if before_chunk is not None:
            before_chunk(k)
        gch = _nn(h2, wg_ref[:, c:c + FFN_CHUNK])
        uch = _nn(h2, wu_ref[:, c:c + FFN_CHUNK])
        acc = acc + _nn((gch * _sigmoid(gch) * uch).astype(BF16), wd_ref[c:c + FFN_CHUNK, :])
        if after_chunk is not None:
            after_chunk(k)
    y_ref[...] = _rms(acc, gf_ref[...]) if final else acc


def _outproj_body(*refs, final):
    _outproj_math(*refs, final=final)


def _outproj_attn_body(pt_ref, *refs, final, layer, npages, nseq):
    proj_refs = refs[:14]
    q_ref, kn_ref, vn_ref, lfn_ref, ckt_hbm, cvt_hbm, clft_hbm, y_ref, os_ref = refs[14:23]
    kring, vring, lring, sems, s_scr, r_scr = refs[23:]
    i = pl.program_id(0)
    sps, tn, _ = q_ref.shape
    hp = npages // RING_SLOTS
    hq = N_HEADS * tn

    def copies(seq, c, parity):
        out = []
        for p in range(hp):
            pid = pt_ref[seq * npages + c * hp + p]
            out.append(pltpu.make_async_copy(ckt_hbm.at[layer, pid], kring.at[c, p], sems.at[c, 0]))
            out.append(pltpu.make_async_copy(cvt_hbm.at[layer, pid], vring.at[c, p], sems.at[c, 1]))
        if c == 0:
            for p in range(npages):
                pid = pt_ref[seq * npages + p]
                out.append(pltpu.make_async_copy(clft_hbm.at[layer, pid], lring.at[parity, p], sems.at[parity, 2]))
        return out

    @pl.when(i == 0)
    def _():
        for c in range(RING_SLOTS):
            for cp in copies(0, c, 0):
                cp.start()

    items = [(j, c) for j in range(sps) for c in range(RING_SLOTS)]
    n_ffn = -(-proj_refs[10].shape[1] // FFN_CHUNK)
    slots = [t * n_ffn // len(items) for t in range(len(items))]
    seq_state = {}
    scored = {}

    def score_phase(j, c):
        seq = i * sps + j
        parity = j % 2
        for cp in copies(seq, c, parity):
            cp.wait()
        if c == 0:
            qbd, own_head, b_new, s_q, causal = _sample_setup(q_ref[j], lring.at[parity], lfn_ref[j], r_scr, npages)
            state = (jnp.full((hq, 1), -jnp.inf, F32), jnp.zeros((hq, 1), F32), jnp.zeros((hq, ATTN_WIDTH), F32))
            new = None
        else:
            qbd, own_head, b_new, s_q, causal, state = seq_state.pop(j)
            zrows = jnp.zeros((PAGE - tn, ATTN_WIDTH), F32)
            kn = jnp.concatenate([kn_ref[j], zrows], axis=0).astype(BF16)
            vn = jnp.concatenate([vn_ref[j], zrows], axis=0).astype(BF16)
            new = (kn, vn, b_new, causal)
        m = _sample_scores(qbd, s_q, kring.at[c], c * hp, r_scr, s_scr, state[0], new)
        scored[(j, c)] = (qbd, own_head, b_new, s_q, causal, state, m, new)

    def accumulate_phase(j, c):
        seq = i * sps + j
        qbd, own_head, b_new, s_q, causal, state, m, new = scored.pop((j, c))
        state = _sample_accumulate(vring.at[c], s_scr, state, m, new)
        for cp in copies(jnp.minimum(seq + 1, nseq - 1), c, (j + 1) % 2):
            cp.start()
        if c == 0:
            seq_state[j] = (qbd, own_head, b_new, s_q, causal, state)
        else:
            _, l, acc = state
            o = jnp.where(own_head, acc / l, 0.0)
            out = o[0:tn, :]
            for h in range(1, N_HEADS):
                out = out + o[h * tn:(h + 1) * tn, :]
            os_ref[j] = out

    def before_chunk(k):
        first = [t for t in range(len(items)) if slots[t] == k][:1]
        for t in first:
            score_phase(*items[t])

    def after_chunk(k):
        mine = [t for t in range(len(items)) if slots[t] == k]
        for n_t, t in enumerate(mine):
            if n_t > 0:
                score_phase(*items[t])
            accumulate_phase(*items[t])

    _outproj_math(*proj_refs, y_ref, final=final, before_chunk=before_chunk, after_chunk=after_chunk)

    @pl.when(i == pl.num_programs(0) - 1)
    def _():
        for c in range(RING_SLOTS):
            for cp in copies(nseq - 1, c, sps % 2):
                cp.wait()


def _outproj_weights(w, gf):
    return [w['wa'], w['wb'], w['wc'], w['wo'], w['norm2_g'], w['wg'], w['wu'], w['wd'], gf]


def _outproj_attn(x2d, za, zb, oc, gates, w, gf, layer, page_table_flat, q, kn, vn, lfn, cache_kt, cache_vt,
                  cache_lft, *, final, tm, npages):
    n, d = x2d.shape
    nseq, tn, _ = q.shape
    steps = n // tm
    assert n % tm == 0 and nseq % steps == 0 and npages % (2 * RING_SLOTS) == 0
    sps = nseq // steps
    assert sps % 2 == 0
    hp = npages // RING_SLOTS
    row = lambda a: pl.BlockSpec((tm, a.shape[1]), lambda i, pt: (i, 0))
    const = lambda a: pl.BlockSpec(a.shape, lambda i, pt: (0,) * a.ndim, pipeline_mode=pl.Buffered(1))
    seqs = lambda width: pl.BlockSpec((sps, tn, width), lambda i, pt: (i, 0, 0))
    seqs_l = lambda width: pl.BlockSpec((None, sps, tn, width), lambda i, pt: (layer, i, 0, 0))
    hbm = pl.BlockSpec(memory_space=pl.ANY)
    weights = _outproj_weights(w, gf)
    grid_spec = pltpu.PrefetchScalarGridSpec(
        num_scalar_prefetch=1,
        grid=(steps,),
        in_specs=[row(x2d), row(za), row(zb), row(oc), row(gates)] + [const(a) for a in weights]
        + [seqs(ATTN_WIDTH), seqs_l(ATTN_WIDTH), seqs_l(ATTN_WIDTH),
           pl.BlockSpec((sps, N_HEADS, PAGE), lambda i, pt: (i, 0, 0)), hbm, hbm, hbm],
        out_specs=[pl.BlockSpec((tm, d), lambda i, pt: (i, 0)), seqs(ATTN_WIDTH)],
        scratch_shapes=[pltpu.VMEM((RING_SLOTS, hp, ATTN_WIDTH, PAGE), F32),
                        pltpu.VMEM((RING_SLOTS, hp, ATTN_WIDTH, PAGE), F32),
                        pltpu.VMEM((2, npages, N_HEADS, PAGE), F32),
                        pltpu.SemaphoreType.DMA((2, 3)),
                        pltpu.VMEM((N_HEADS * tn, (hp + 1) * PAGE), F32),
                        pltpu.VMEM(((npages + 1) * N_HEADS, PAGE), F32)],
    )
    return pl.pallas_call(
        functools.partial(_outproj_attn_body, final=final, layer=layer, npages=npages, nseq=nseq),
        grid_spec=grid_spec,
        out_shape=[jax.ShapeDtypeStruct((n, d), F32), jax.ShapeDtypeStruct((nseq, tn, ATTN_WIDTH), F32)],
        compiler_params=pltpu.CompilerParams(dimension_semantics=("arbitrary",), vmem_limit_bytes=VMEM_LIMIT),
        name="outproj_attn_final" if final else "outproj_attn",
    )(page_table_flat, x2d, za, zb, oc, gates, *weights, q, kn, vn, lfn, cache_kt, cache_vt, cache_lft)


def _outproj(x2d, za, zb, oc, gates, w, gf, *, final, tm):
    n, d = x2d.shape
    assert n % tm == 0
    row = lambda a: pl.BlockSpec((tm, a.shape[1]), lambda i: (i, 0))
    weights = _outproj_weights(w, gf)
    return pl.pallas_call(
        functools.partial(_outproj_body, final=final),
        grid=(n // tm,),
        in_specs=[row(x2d), row(za), row(zb), row(oc), row(gates)] + [_const_spec(a.shape) for a in weights],
        out_specs=pl.BlockSpec((tm, d), lambda i: (i, 0)),
        out_shape=jax.ShapeDtypeStruct((n, d), F32),
        compiler_params=pltpu.CompilerParams(dimension_semantics=("arbitrary",), vmem_limit_bytes=VMEM_LIMIT),
        name="outproj_final" if final else "outproj",
    )(x2d, za, zb, oc, gates, *weights)


def _layer_weights(l, norm1_g, w_in, b_f, conv_a_w, conv_b_w, conv_b_bias, cf_norm_g, cf_norm_b,
                   w_a_out, w_b_out, w_c_out, w_o, norm2_g, w_ffn_gate, w_ffn_up, w_ffn_down):
    d = w_in.shape[1]
    wi = w_in[l]
    off_f = 3 * ATTN_WIDTH
    off_mix = off_f + N_HEADS
    off_gate = off_mix + 3 * SC_WIDTH + 2 * CF_WIDTH
    wq, wk, wv = (wi[:, i * ATTN_WIDTH:(i + 1) * ATTN_WIDTH].astype(BF16) for i in range(3))
    pad = LANES - N_HEADS
    return {
        'norm1_g': norm1_g[l].reshape(1, d),
        'wq': wq, 'wk': wk, 'wv': wv, 'wq_t': wq.T, 'wk_t': wk.T, 'wv_t': wv.T,
        'wf': jnp.pad(wi[:, off_f:off_mix], ((0, 0), (0, pad))).astype(BF16),
        'bf': jnp.pad(b_f[l], (0, pad)).reshape(1, LANES),
        'wmix': wi[:, off_mix:off_gate].astype(BF16),
        'wgate': wi[:, off_gate:].astype(BF16),
        'conv_a_w': conv_a_w[l], 'conv_b_w': conv_b_w[l], 'conv_b_bias': conv_b_bias[l].reshape(1, -1),
        'cf_norm_g': cf_norm_g[l].reshape(1, -1), 'cf_norm_b': cf_norm_b[l].reshape(1, -1),
        'wa': w_a_out[l].astype(BF16), 'wb': w_b_out[l].astype(BF16), 'wc': w_c_out[l].astype(BF16),
        'wo': w_o[l].astype(BF16), 'norm2_g': norm2_g[l].reshape(1, d),
        'wg': w_ffn_gate[l].astype(BF16), 'wu': w_ffn_up[l].astype(BF16), 'wd': w_ffn_down[l].astype(BF16),
    }


def kernel(x_prompt, x_sample, cache_k, cache_v, cache_logf, state_conv_a, state_conv_b, page_table, norm1_g, w_in, b_f, conv_a_w, conv_b_w, conv_b_bias, cf_norm_g, cf_norm_b, w_a_out, w_b_out, w_c_out, w_o, norm2_g, w_ffn_gate, w_ffn_up, w_ffn_down, final_norm_g):
    depth = w_in.shape[0]
    pb, ps, d = x_prompt.shape
    sb_, st, _ = x_sample.shape
    n_pool = cache_k.shape[1]
    npages = page_table.shape[1]
    assert cache_k.shape[2:] == (PAGE, N_HEADS, HEAD_DIM) and ps % PAGE == 0
    cache_kt = cache_k.transpose(0, 1, 3, 4, 2).reshape(depth, n_pool, ATTN_WIDTH, PAGE)
    cache_vt = cache_v.transpose(0, 1, 3, 4, 2).reshape(depth, n_pool, ATTN_WIDTH, PAGE)
    cache_lft = cache_logf.transpose(0, 1, 3, 2)
    pt_flat = page_table.reshape(-1)
    gf = final_norm_g.reshape(1, d)
    tm_p = 512 if (pb * ps) % 512 == 0 else PAGE
    tm_s = 512 if (sb_ * st) % 512 == 0 else sb_ * st
    tq = 256 if ps % 256 == 0 else PAGE
    bb = 16 if sb_ % 16 == 0 else sb_
    conv_rows = 64 if ps % 64 == 0 else ps

    xp = x_prompt.reshape(pb * ps, d)
    xs = x_sample.reshape(sb_ * st, d)
    zeros_a = jnp.zeros((pb, SC_KERNEL - 1, SC_WIDTH), F32)
    zeros_b = jnp.zeros((pb, CF_KERNEL - 1, CF_WIDTH), F32)
    outs = {k: [] for k in ('ap', 'bp', 'fs', 'as', 'bs')}
    prev_p = prev_s = None
    npp = ps // PAGE
    for l in range(depth):
        w = _layer_weights(l, norm1_g, w_in, b_f, conv_a_w, conv_b_w, conv_b_bias, cf_norm_g, cf_norm_b,
                           w_a_out, w_b_out, w_c_out, w_o, norm2_g, w_ffn_gate, w_ffn_up, w_ffn_down)
        final = l == depth - 1
        q, k, v, _, _, lf, sbs, ua, ub, gates_s = _inproj(xs, w, prev_s, paged=False, tm=tm_s)
        prev_s = (k, v)
        za_s, zb_s, na_s, nb_s = _mixers(ua.reshape(sb_, st, -1), ub.reshape(sb_, st, -1), sbs.reshape(sb_, st, -1),
                                         state_conv_a[l], state_conv_b[l], w, bb=bb, rows=st)
        lf = lf[:, :N_HEADS].reshape(sb_, st, N_HEADS)
        lfn = jnp.pad(lf.transpose(0, 2, 1), ((0, 0), (0, 0), (0, PAGE - st)))
        qt, kt, vt, kb, vtb, lft, sbp, ua, ub, gates = _inproj(xp, w, prev_p, paged=True, tm=tm_p)
        prev_p = (kt, vt, lft)
        za, zb, na, nb = _mixers(ua.reshape(pb, ps, -1), ub.reshape(pb, ps, -1), sbp.reshape(pb, ps, -1),
                                 zeros_a, zeros_b, w, bb=1, rows=conv_rows)
        oc = _prompt_attn(qt, kb, vtb, lft, batch=pb, tq=tq)
        xp, oc_s = _outproj_attn(xp, za.reshape(pb * ps, -1), zb.reshape(pb * ps, -1), oc, gates, w, gf, l, pt_flat,
                                 q.reshape(sb_, st, -1), k.reshape(l + 1, sb_, st, -1), v.reshape(l + 1, sb_, st, -1),
                                 lfn, cache_kt, cache_vt, cache_lft, final=final, tm=tm_p, npages=npages)
        xs = _outproj(xs, za_s.reshape(sb_ * st, -1), zb_s.reshape(sb_ * st, -1), oc_s.reshape(sb_ * st, -1), gates_s,
                      w, gf, final=final, tm=tm_s)
        outs['ap'].append(na)
        outs['bp'].append(nb)
        outs['fs'].append(lf)
        outs['as'].append(na_s)
        outs['bs'].append(nb_s)
    stack = lambda k: jnp.stack(outs[k])
    kt, vt, lft = prev_p
    to_pages = lambda a: a.reshape(depth, pb, npp, N_HEADS, HEAD_DIM, PAGE).transpose(0, 1, 2, 5, 3, 4)
    ks, vs = (a.reshape(depth, sb_, st, N_HEADS, HEAD_DIM) for a in prev_s)
    return (xp.reshape(pb, ps, d), xs.reshape(sb_, st, d), to_pages(kt), to_pages(vt),
            lft.reshape(depth, pb, npp, N_HEADS, PAGE).transpose(0, 1, 2, 4, 3), stack('ap'), stack('bp'),
            ks, vs, stack('fs'), stack('as'), stack('bs'))
```

```python
import functools

import jax
import jax.numpy as jnp
from jax import lax
from jax.experimental import pallas as pl
from jax.experimental.pallas import tpu as pltpu

F32 = jnp.float32
BF16 = jnp.bfloat16

EPS = 1e-6
LOG2E = 1.4426950408889634
N_HEADS = 8
HEAD_DIM = 64
ATTN_WIDTH = N_HEADS * HEAD_DIM
PAGE = 128
SC_WIDTH = 256
SC_KERNEL = 3
CF_WIDTH = 256
CF_KERNEL = 31
N_BRANCHES = 3
LANES = 128
VMEM_LIMIT = 56 * 1024 * 1024


def _nt(a, b):
    return lax.dot_general(a, b, (((1,), (1,)), ((), ())), preferred_element_type=F32)


def _nn(a, b):
    return jnp.dot(a, b, preferred_element_type=F32)


def _split3(x):
    hi = x.astype(BF16)
    r1 = x - hi.astype(F32)
    mid = r1.astype(BF16)
    lo = (r1 - mid.astype(F32)).astype(BF16)
    return hi, mid, lo


def _exact_nn(x, w01):
    hi, mid, lo = _split3(x)
    return _nn(hi, w01) + _nn(mid, w01) + _nn(lo, w01)


def _sigmoid(x):
    return 1.0 / (1.0 + jnp.exp(-x))


def _log_sigmoid(x):
    return jnp.minimum(x, 0.0) - jnp.log(1.0 + jnp.exp(-jnp.abs(x)))


def _rms(x, g):
    return x * lax.rsqrt(jnp.mean(x * x, axis=-1, keepdims=True) + EPS) * g


def _const_spec(shape):
    nd = len(shape)
    return pl.BlockSpec(shape, lambda *_: (0,) * nd, pipeline_mode=pl.Buffered(1))


def _inproj_body(r, *, paged, tiles_per_seq, conv_rows):
    h = _rms(r['x'][...], r['g'][...]).astype(BF16)
    tm = h.shape[0]
    last = r['k'].shape[0] - 1
    for j in range(last):
        r['k'][j] = r['k_prev'][j]
        r['v'][j] = r['v_prev'][j]
        if paged:
            r['lf'][j] = r['lf_prev'][j]

    def mix(c0, c1):
        return _nn(h, r['wmix'][:, c0:c1])

    gw = 512
    gate_cols = list(range(0, r['gate'].shape[1], gw))

    def gate(c):
        r['gate'][:, c:c + gw] = _sigmoid(_nn(h, r['wgate'][:, c:c + gw])).astype(BF16)

    f = _nn(h, r['wf'][...]) + r['bf'][...]
    lf = _log_sigmoid(f)
    sb = mix(0, SC_WIDTH)
    ua = mix(SC_WIDTH, 2 * SC_WIDTH) * mix(2 * SC_WIDTH, 3 * SC_WIDTH)
    c0 = 3 * SC_WIDTH
    ub = mix(c0, c0 + CF_WIDTH) * _sigmoid(mix(c0 + CF_WIDTH, c0 + 2 * CF_WIDTH))
    if paged:
        qt = (_nt(r['wq'][...], h) * (HEAD_DIM ** -0.5 * LOG2E)).astype(BF16)
        kt = _nt(r['wk'][...], h)
        vt = _nt(r['wv'][...], h)
        r['kb'][...] = kt.T.astype(BF16)
        lft = lf.T
        for p in range(tm // PAGE):
            sl = slice(p * PAGE, (p + 1) * PAGE)
            r['q'][p] = qt[:, sl]
            r['k'][last, p] = kt[:, sl]
            r['v'][last, p] = vt[:, sl]
            r['vb'][p] = vt[:, sl].astype(BF16)
            r['lf'][last, p] = lft[:N_HEADS, sl]
        sa, sbuf, sb_scr = r['sa'], r['sbuf'], r['sb_scr']
        a0 = A_PAD - (SC_KERNEL - 1)
        b0 = B_PAD - (CF_KERNEL - 1)
        i = pl.program_id(0)

        @pl.when(i == 0)
        def _():
            sa[...] = jnp.zeros(sa.shape, F32)
            sbuf[...] = jnp.zeros(sbuf.shape, F32)

        first = (i % tiles_per_seq) == 0
        sa[:, a0:A_PAD, :] = jnp.where(first, 0.0, sa[:, tm + a0:tm + A_PAD, :])
        sbuf[:, b0:B_PAD, :] = jnp.where(first, 0.0, sbuf[:, tm + b0:tm + B_PAD, :])
        sa[0, A_PAD:A_PAD + tm, :] = ua
        sbuf[0, B_PAD:B_PAD + tm, :] = ub
        sb_scr[0] = sb
        r['na'][...] = sa[:, tm + a0:tm + A_PAD, :]
        r['nb'][...] = sbuf[:, tm + b0:tm + B_PAD, :]
        n_conv = tm // conv_rows
        gate_after = {}
        for g in range(len(gate_cols)):
            gate_after.setdefault(g * n_conv // len(gate_cols), []).append(gate_cols[g])

        def after_chunk(k):
            for c in gate_after.get(k, []):
                gate(c)

        _conv_chunks(sa, sbuf, lambda r0: sb_scr[:, pl.ds(r0, conv_rows), :], r['conv_a_w'], r['conv_b_w'],
                     r['conv_b_bias'][...], r['cf_norm_g'][...], r['cf_norm_b'][...], r['za'], r['zb'],
                     t=tm, rows=conv_rows, unrolled=True, after_chunk=after_chunk)
    else:
        r['q'][...] = _nn(h, r['wq'][...]) * (HEAD_DIM ** -0.5)
        r['k'][last] = _nn(h, r['wk'][...])
        r['v'][last] = _nn(h, r['wv'][...])
        r['lf'][...] = lf
        r['sb'][...] = sb.astype(BF16)
        r['ua'][...] = ua
        r['ub'][...] = ub
        for c in gate_cols:
            gate(c)


def _inproj(x2d, w, prev, *, paged, tm, tiles_per_seq=1, conv_rows=64):
    n, d = x2d.shape
    assert n % tm == 0 and tm % PAGE == 0
    steps = n // tm
    layers = 1 if prev is None else prev[0].shape[0] + 1
    row = lambda width, dt: (jax.ShapeDtypeStruct((n, width), dt), pl.BlockSpec((tm, width), lambda i: (i, 0)))
    row_l = lambda nl, width, dt: (jax.ShapeDtypeStruct((nl, n, width), dt),
                                   pl.BlockSpec((nl, tm, width), lambda i: (0, i, 0)))
    const = lambda a: (a, _const_spec(a.shape))
    ins = {'x': (x2d, pl.BlockSpec((tm, d), lambda i: (i, 0))), 'g': const(w['norm1_g']), 'wf': const(w['wf']),
           'bf': const(w['bf']), 'wmix': const(w['wmix']), 'wgate': const(w['wgate'])}
    scratch = {}
    if paged:
        npg = tm // PAGE
        pg = lambda rows, dt: (jax.ShapeDtypeStruct((n // PAGE, rows, PAGE), dt),
                               pl.BlockSpec((npg, rows, PAGE), lambda i: (i, 0, 0)))
        pg_l = lambda nl, rows, dt: (jax.ShapeDtypeStruct((nl, n // PAGE, rows, PAGE), dt),
                                     pl.BlockSpec((nl, npg, rows, PAGE), lambda i: (0, i, 0, 0)))
        tile = lambda width: (jax.ShapeDtypeStruct((steps, tm, width), BF16),
                              pl.BlockSpec((1, tm, width), lambda i: (i, 0, 0)))
        state = lambda rows, width: (jax.ShapeDtypeStruct((steps // tiles_per_seq, rows, width), F32),
                                     pl.BlockSpec((1, rows, width), lambda i: (i // tiles_per_seq, 0, 0)))
        for name in ('wq', 'wk', 'wv'):
            ins[name] = const(w[name + '_t'])
        for name in ('conv_a_w', 'conv_b_w', 'conv_b_bias', 'cf_norm_g', 'cf_norm_b'):
            ins[name] = const(w[name])
        outs = {'q': pg(ATTN_WIDTH, BF16), 'k': pg_l(layers, ATTN_WIDTH, F32), 'v': pg_l(layers, ATTN_WIDTH, F32),
                'kb': row(ATTN_WIDTH, BF16), 'vb': pg(ATTN_WIDTH, BF16), 'lf': pg_l(layers, N_HEADS, F32),
                'za': tile(SC_WIDTH), 'zb': tile(CF_WIDTH), 'na': state(SC_KERNEL - 1, SC_WIDTH),
                'nb': state(CF_KERNEL - 1, CF_WIDTH)}
        prev_specs = {'k_prev': pg_l(layers - 1, ATTN_WIDTH, F32), 'v_prev': pg_l(layers - 1, ATTN_WIDTH, F32),
                      'lf_prev': pg_l(layers - 1, N_HEADS, F32)}
        scratch = {'sa': pltpu.VMEM((1, tm + A_WIN, SC_WIDTH), F32), 'sbuf': pltpu.VMEM((1, tm + B_WIN, CF_WIDTH), F32),
                   'sb_scr': pltpu.VMEM((1, tm, SC_WIDTH), F32)}
    else:
        for name in ('wq', 'wk', 'wv'):
            ins[name] = const(w[name])
        outs = {'q': row(ATTN_WIDTH, F32), 'k': row_l(layers, ATTN_WIDTH, F32), 'v': row_l(layers, ATTN_WIDTH, F32),
                'lf': row(LANES, F32), 'sb': row(SC_WIDTH, BF16), 'ua': row(SC_WIDTH, F32), 'ub': row(CF_WIDTH, F32)}
        prev_specs = {'k_prev': row_l(layers - 1, ATTN_WIDTH, F32), 'v_prev': row_l(layers - 1, ATTN_WIDTH, F32)}
    outs['gate'] = row(N_BRANCHES * d, BF16)
    if prev is not None:
        for name, arr in zip(prev_specs, prev):
            ins[name] = (arr, prev_specs[name][1])
    names = list(ins) + list(outs) + list(scratch)

    def body(*refs):
        _inproj_body(dict(zip(names, refs)), paged=paged, tiles_per_seq=tiles_per_seq, conv_rows=conv_rows)

    res = pl.pallas_call(
        body,
        grid=(steps,),
        in_specs=[v[1] for v in ins.values()],
        out_specs=[v[1] for v in outs.values()],
        out_shape=[v[0] for v in outs.values()],
        scratch_shapes=list(scratch.values()),
        compiler_params=pltpu.CompilerParams(dimension_semantics=("arbitrary",), vmem_limit_bytes=VMEM_LIMIT),
        name="inproj_paged" if paged else "inproj_rows",
    )(*[v[0] for v in ins.values()])
    return dict(zip(outs, res))


SUBLANES = 8
A_PAD = 8
B_PAD = 32
A_WIN = 8
B_WIN = 40


def _shift_rows(x, first, n):
    if first % SUBLANES == 0:
        return x[:, first:first + n, :]
    total = x.shape[1]
    return pltpu.roll(x, total - first, 1)[:, :n, :]


def _mixers_body(ua_ref, ub_ref, sb_ref, ha_ref, hb_ref, wa_ref, wb_ref, bias_ref, lng_ref, lnb_ref,
                 za_ref, zb_ref, na_ref, nb_ref, sa, sbuf, *, rows):
    bb, t, _ = ua_ref.shape
    a0 = A_PAD - (SC_KERNEL - 1)
    b0 = B_PAD - (CF_KERNEL - 1)
    sa[:, a0:A_PAD, :] = ha_ref[...]
    sa[:, A_PAD:A_PAD + t, :] = ua_ref[...]
    sbuf[:, b0:B_PAD, :] = hb_ref[...]
    sbuf[:, B_PAD:B_PAD + t, :] = ub_ref[...]
    sbuf[:, B_PAD + t:, :] = jnp.zeros((bb, B_WIN - B_PAD, CF_WIDTH), F32)
    sa[:, 0:a0, :] = jnp.zeros((bb, a0, SC_WIDTH), F32)
    sbuf[:, 0:b0, :] = jnp.zeros((bb, b0, CF_WIDTH), F32)
    na_ref[...] = sa[:, t + a0:t + A_PAD, :]
    nb_ref[...] = sbuf[:, t + b0:t + B_PAD, :]
    _conv_chunks(sa, sbuf, lambda r0: sb_ref[:, pl.ds(r0, rows), :].astype(F32), wa_ref, wb_ref, bias_ref[...],
                 lng_ref[...], lnb_ref[...], za_ref, zb_ref, t=t, rows=rows, unrolled=False)


def _conv_chunks(sa, sbuf, sb_at, wa_ref, wb_ref, bias, lng, lnb, za_ref, zb_ref, *, t, rows, unrolled,
                 after_chunk=None):
    a0 = A_PAD - (SC_KERNEL - 1)
    b0 = B_PAD - (CF_KERNEL - 1)

    def convs(r0):
        halves = []
        for c in range(0, CF_WIDTH, LANES):
            win = sbuf[:, pl.ds(r0, rows + B_WIN), c:c + LANES]
            acc = None
            for b in range(SUBLANES):
                n_a = (CF_KERNEL - 1 - b) // SUBLANES + 1
                shifted = _shift_rows(win, b0 + b, rows + SUBLANES * (n_a - 1))
                for a in range(n_a):
                    i = SUBLANES * a + b
                    term = wb_ref[i:i + 1, c:c + LANES] * shifted[:, SUBLANES * a:SUBLANES * a + rows, :]
                    acc = term if acc is None else acc + term
            halves.append(acc)
        win_a = sa[:, pl.ds(r0, rows + A_WIN), :]
        acc_a = wa_ref[0:1, :] * _shift_rows(win_a, a0, rows)
        for i in range(1, SC_KERNEL):
            acc_a = acc_a + wa_ref[i:i + 1, :] * _shift_rows(win_a, a0 + i, rows)
        za_ref[:, pl.ds(r0, rows), :] = (sb_at(r0) * acc_a).astype(BF16)
        return jnp.concatenate(halves, axis=-1)

    def norm_act(z, r0):
        z = z + bias
        mu = jnp.mean(z, axis=-1, keepdims=True)
        zc = z - mu
        var = jnp.mean(zc * zc, axis=-1, keepdims=True)
        y = zc * lax.rsqrt(var + EPS) * lng + lnb
        zb_ref[:, pl.ds(r0, rows), :] = (y * _sigmoid(y)).astype(BF16)

    if t == rows:
        norm_act(convs(0), 0)
    elif unrolled:
        z_prev = convs(0)
        for r0 in range(rows, t, rows):
            if after_chunk is not None:
                after_chunk(r0 // rows - 1)
            z = convs(r0)
            norm_act(z_prev, r0 - rows)
            z_prev = z
        norm_act(z_prev, t - rows)
        if after_chunk is not None:
            after_chunk(t // rows - 1)
    else:
        def step(c, z_prev):
            r0 = pl.multiple_of(c * rows, rows)
            z = convs(r0)
            norm_act(z_prev, pl.multiple_of((c - 1) * rows, rows))
            return z
        z_last = lax.fori_loop(1, t // rows, step, convs(0))
        norm_act(z_last, t - rows)


def _mixers(ua, ub, sb, hist_a, hist_b, w, *, bb, rows):
    b, t, _ = ua.shape
    assert b % bb == 0 and t % rows == 0
    seq = lambda width: pl.BlockSpec((bb, t, width), lambda i: (i, 0, 0))
    hist = lambda r, width: pl.BlockSpec((bb, r, width), lambda i: (i, 0, 0))
    consts = [w['conv_a_w'], w['conv_b_w'], w['conv_b_bias'], w['cf_norm_g'], w['cf_norm_b']]
    return pl.pallas_call(
        functools.partial(_mixers_body, rows=rows),
        grid=(b // bb,),
        in_specs=[seq(SC_WIDTH), seq(CF_WIDTH), seq(SC_WIDTH), hist(SC_KERNEL - 1, SC_WIDTH),
                  hist(CF_KERNEL - 1, CF_WIDTH)] + [_const_spec(a.shape) for a in consts],
        out_specs=[seq(SC_WIDTH), seq(CF_WIDTH), hist(SC_KERNEL - 1, SC_WIDTH), hist(CF_KERNEL - 1, CF_WIDTH)],
        out_shape=[jax.ShapeDtypeStruct((b, t, SC_WIDTH), BF16), jax.ShapeDtypeStruct((b, t, CF_WIDTH), BF16),
                   jax.ShapeDtypeStruct((b, SC_KERNEL - 1, SC_WIDTH), F32),
                   jax.ShapeDtypeStruct((b, CF_KERNEL - 1, CF_WIDTH), F32)],
        scratch_shapes=[pltpu.VMEM((bb, t + A_WIN, SC_WIDTH), F32), pltpu.VMEM((bb, t + B_WIN, CF_WIDTH), F32)],
        compiler_params=pltpu.CompilerParams(dimension_semantics=("arbitrary",), vmem_limit_bytes=VMEM_LIMIT),
        name=f"mixers_t{t}",
    )(ua, ub, sb, hist_a, hist_b, *consts)


def _tri(n, fn):
    r = lax.broadcasted_iota(jnp.int32, (n, n), 0)
    c = lax.broadcasted_iota(jnp.int32, (n, n), 1)
    return jnp.where(fn(r, c), 1.0, 0.0).astype(BF16)


def _prompt_attn_body(qt_ref, k_ref, vt_ref, lft_ref, o_ref, ct_scr, ccol_scr, crep_scr, m_scr, l_scr, acc_scr,
                      *, tq, heads_per_step):
    npages = qt_ref.shape[0]
    t = npages * PAGE
    ppb = tq // PAGE
    x = lft_ref[...].reshape(npages * N_HEADS, PAGE)
    cw = _exact_nn(x, _tri(PAGE, lambda r, c: r <= c))
    run = jnp.zeros((N_HEADS, 1), F32)
    zpad = jnp.zeros((LANES - N_HEADS, PAGE), F32)
    for p in range(npages):
        cp = cw[p * N_HEADS:(p + 1) * N_HEADS, :] + run
        run = cp[:, PAGE - 1:PAGE]
        cp2 = cp * LOG2E
        ct_scr[p] = cp2
        ccol_scr[p * PAGE:(p + 1) * PAGE, :] = jnp.concatenate([cp2, zpad], axis=0).T
    zq = jnp.zeros((HEAD_DIM, tq), BF16)
    key_i = lax.broadcasted_iota(jnp.int32, (tq, tq), 0)
    qry_i = lax.broadcasted_iota(jnp.int32, (tq, tq), 1)
    for h in range(N_HEADS):
        for r in range(0, t, tq):
            crep_scr[h, r:r + tq, :] = jnp.broadcast_to(ccol_scr[r:r + tq, h:h + 1], (tq, LANES))

    def paged(ref, blk, rws):
        return jnp.concatenate([ref[blk * ppb + j, rws, :] for j in range(ppb)], axis=1)

    for h0 in range(0, N_HEADS, heads_per_step):
        heads = range(h0, h0 + heads_per_step)

        def q_block(qi, carry):
            q0 = pl.multiple_of(qi * tq, tq)
            for h in heads:
                m_scr[h] = jnp.full((1, tq), -jnp.inf, F32)
                l_scr[h] = jnp.zeros((1, tq), F32)
                acc_scr[h] = jnp.zeros((HEAD_DIM, tq), F32)

            def k_step(blocks):
                s, pe, alpha = {}, {}, {}

                def scores(kj, masked, h):
                    k0 = pl.multiple_of(kj * tq, tq)
                    pair = slice((h // 2) * 2 * HEAD_DIM, (h // 2 + 1) * 2 * HEAD_DIM)
                    qh = paged(qt_ref, qi, slice(h * HEAD_DIM, (h + 1) * HEAD_DIM))
                    qtz = jnp.concatenate([qh, zq] if h % 2 == 0 else [zq, qh], axis=0)
                    cs = crep_scr[h, pl.ds(k0, tq), :]
                    sh = _nn(k_ref[pl.ds(k0, tq), pair], qtz) - jnp.concatenate([cs] * (tq // LANES), axis=1)
                    return jnp.where(key_i <= qry_i, sh, -jnp.inf) if masked else sh

                def softmax(sh, h):
                    ct = jnp.concatenate([ct_scr[qi * ppb + j, h:h + 1, :] for j in range(ppb)], axis=1)
                    m = m_scr[h]
                    m_new = jnp.maximum(m, jnp.max(sh, axis=0, keepdims=True) + ct)
                    a = jnp.exp2(m - m_new)
                    p = jnp.exp2(sh + (ct - m_new))
                    m_scr[h] = m_new
                    l_scr[h] = a * l_scr[h] + jnp.sum(p, axis=0, keepdims=True)
                    return a, p.astype(BF16)

                def weighted_values(kj, h, a, p):
                    vt = paged(vt_ref, kj, slice(h * HEAD_DIM, (h + 1) * HEAD_DIM))
                    acc_scr[h] = acc_scr[h] * a + _nn(vt, p)

                todo = [(kj, masked, h) for kj, masked in blocks for h in heads]
                for i in range(len(todo) + 2):
                    if i < len(todo):
                        s[i] = scores(*todo[i])
                    if 0 <= i - 2 < len(todo):
                        weighted_values(todo[i - 2][0], todo[i - 2][2], alpha.pop(i - 2), pe.pop(i - 2))
                    if 0 <= i - 1 < len(todo):
                        alpha[i - 1], pe[i - 1] = softmax(s.pop(i - 1), todo[i - 1][2])

            def pair_step(kp, c):
                k_step([(2 * kp, False), (2 * kp + 1, False)])
                return c

            def single_step(kj, c):
                k_step([(kj, False)])
                return c

            lax.fori_loop(0, qi // 2, pair_step, 0)
            lax.fori_loop(2 * (qi // 2), qi, single_step, 0)
            k_step([(qi, True)])
            for hp in range(h0 // 2, (h0 + heads_per_step) // 2):
                ot = jnp.concatenate([acc_scr[2 * hp + e] / l_scr[2 * hp + e] for e in range(2)], axis=0)
                o_ref[pl.ds(q0, tq), hp * 2 * HEAD_DIM:(hp + 1) * 2 * HEAD_DIM] = ot.T.astype(o_ref.dtype)
            return carry

        lax.fori_loop(0, t // tq, q_block, 0)


ATTN_HEADS_PER_STEP = 8


def _prompt_attn(qt, kb, vtb, lft_layers, *, batch, tq):
    n, _ = kb.shape
    t = n // batch
    npages = t // PAGE
    layer = lft_layers.shape[0] - 1
    assert t % tq == 0 and tq % PAGE == 0
    paged_spec = pl.BlockSpec((npages, ATTN_WIDTH, PAGE), lambda b: (b, 0, 0))
    return pl.pallas_call(
        functools.partial(_prompt_attn_body, tq=tq, heads_per_step=ATTN_HEADS_PER_STEP),
        grid=(batch,),
        in_specs=[paged_spec, pl.BlockSpec((t, ATTN_WIDTH), lambda b: (b, 0)), paged_spec,
                  pl.BlockSpec((None, npages, N_HEADS, PAGE), lambda b: (layer, b, 0, 0))],
        out_specs=pl.BlockSpec((t, ATTN_WIDTH), lambda b: (b, 0)),
        out_shape=jax.ShapeDtypeStruct((n, ATTN_WIDTH), BF16),
        scratch_shapes=[pltpu.VMEM((npages, N_HEADS, PAGE), F32), pltpu.VMEM((t, LANES), F32),
                        pltpu.VMEM((N_HEADS, t, LANES), F32), pltpu.VMEM((N_HEADS, 1, tq), F32),
                        pltpu.VMEM((N_HEADS, 1, tq), F32), pltpu.VMEM((N_HEADS, HEAD_DIM, tq), F32)],
        compiler_params=pltpu.CompilerParams(dimension_semantics=("arbitrary",), vmem_limit_bytes=VMEM_LIMIT),
        name="prompt_attn",
    )(qt, kb, vtb, lft_layers)


def _bias_rows(r_scr, blk, tn):
    return jnp.concatenate(
        [jnp.broadcast_to(r_scr[blk * N_HEADS + h:blk * N_HEADS + h + 1, :], (tn, PAGE)) for h in range(N_HEADS)],
        axis=0)


def _sample_setup(q, lft_ref, lfn, r_scr, npages):
    tn = q.shape[0]
    hq = N_HEADS * tn
    row = lax.broadcasted_iota(jnp.int32, (hq, ATTN_WIDTH), 0)
    lane = lax.broadcasted_iota(jnp.int32, (hq, ATTN_WIDTH), 1)
    own_head = (lane // HEAD_DIM) == (row // tn)
    qbd = jnp.where(own_head, jnp.concatenate([q] * N_HEADS, axis=0), 0.0).astype(BF16)
    nblk = npages + 1
    x = jnp.concatenate([lft_ref[p] for p in range(npages)] + [lfn], axis=0)
    rw = _exact_nn(x, _tri(PAGE, lambda r, c: r > c))
    tot = rw[:, 0:1] + x[:, 0:1]
    run = jnp.zeros((N_HEADS, 1), F32)
    for p in reversed(range(nblk)):
        hs = slice(p * N_HEADS, (p + 1) * N_HEADS)
        r_scr[hs, :] = rw[hs, :] + run
        run = run + tot[hs, :]
    row_p = lax.broadcasted_iota(jnp.int32, (hq, PAGE), 0) % tn
    lane_p = lax.broadcasted_iota(jnp.int32, (hq, PAGE), 1)
    b_new = _bias_rows(r_scr, npages, tn)
    s_q = jnp.sum(jnp.where(lane_p == row_p, b_new, 0.0), axis=-1, keepdims=True)
    return qbd, own_head, b_new, s_q, lane_p <= row_p


def _sample_scores(qbd, s_q, k_pages, page0, r_scr, s_scr, m_prev, new):
    hq = qbd.shape[0]
    tn = hq // N_HEADS
    hp = k_pages.shape[0]
    m = m_prev
    for pp in range(0, hp, 2):
        kt2 = jnp.concatenate([k_pages[pp], k_pages[pp + 1]], axis=1).astype(BF16)
        bias = jnp.concatenate([_bias_rows(r_scr, page0 + pp, tn), _bias_rows(r_scr, page0 + pp + 1, tn)], axis=1)
        s = _nn(qbd, kt2) + bias - s_q
        s_scr[:, pp * PAGE:(pp + 2) * PAGE] = s
        m = jnp.maximum(m, jnp.max(s, axis=-1, keepdims=True))
    if new is not None:
        kn, vn, b_new, causal = new
        s = jnp.where(causal, _nt(qbd, kn) + b_new - s_q, -jnp.inf)
        s_scr[:, hp * PAGE:(hp + 1) * PAGE] = s
        m = jnp.maximum(m, jnp.max(s, axis=-1, keepdims=True))
    return m


def _sample_accumulate(v_pages, s_scr, state, m, new):
    m_prev, l_prev, acc_prev = state
    hp = v_pages.shape[0]
    alpha = jnp.exp(m_prev - m)
    l = alpha * l_prev
    acc = alpha * acc_prev
    for pp in range(0, hp, 2):
        pe = jnp.exp(s_scr[:, pp * PAGE:(pp + 2) * PAGE] - m)
        l = l + jnp.sum(pe, axis=-1, keepdims=True)
        vt2 = jnp.concatenate([v_pages[pp], v_pages[pp + 1]], axis=1).astype(BF16)
        acc = acc + _nt(pe.astype(BF16), vt2)
    if new is not None:
        vn = new[1]
        pe = jnp.exp(s_scr[:, hp * PAGE:(hp + 1) * PAGE] - m)
        l = l + jnp.sum(pe, axis=-1, keepdims=True)
        acc = acc + _nn(pe.astype(BF16), vn)
    return m, l, acc


FFN_CHUNK = 256
RING_SLOTS = 2
PAGE_DMA_PRIORITY = 1


def _outproj_math(x_ref, za_ref, zb_ref, oc_ref, gate_ref, wa_ref, wb_ref, wc_ref, wo_ref, g2_ref,
                  wg_ref, wu_ref, wd_ref, gf_ref, y_ref, *, final, before_chunk=None, after_chunk=None):
    d = x_ref.shape[1]
    ya = _nn(za_ref[...].astype(BF16), wa_ref[...])
    yb = _nn(zb_ref[...].astype(BF16), wb_ref[...])
    yc = _nn(oc_ref[...].astype(BF16), wc_ref[...])
    merged = (gate_ref[:, 0:d].astype(F32) * ya + gate_ref[:, d:2 * d].astype(F32) * yb
              + gate_ref[:, 2 * d:3 * d].astype(F32) * yc)
    x1 = x_ref[...] + _nn(merged.astype(BF16), wo_ref[...])
    h2 = _rms(x1, g2_ref[...]).astype(BF16)
    acc = x1
    for k, c in enumerate(range(0, wg_ref.shape[1], FFN_CHUNK)):
        if before_chunk is not None:
            before_chunk(k)
        gch = _nn(h2, wg_ref[:, c:c + FFN_CHUNK])
        uch = _nn(h2, wu_ref[:, c:c + FFN_CHUNK])
        acc = acc + _nn((gch * _sigmoid(gch) * uch).astype(BF16), wd_ref[c:c + FFN_CHUNK, :])
        if after_chunk is not None:
            after_chunk(k)
    y_ref[...] = _rms(acc, gf_ref[...]) if final else acc


def _outproj_body(*refs, final):
    _outproj_math(*refs, final=final)


def _outproj_attn_body(pt_ref, *refs, final, layer, npages, nseq):
    proj_refs = refs[:14]
    q_ref, kn_ref, vn_ref, lfn_ref, ckt_hbm, cvt_hbm, clft_hbm, y_ref, os_ref = refs[14:23]
    kring, vring, lring, sems, s_scr, r_scr = refs[23:]
    i = pl.program_id(0)
    sps, tn, _ = q_ref.shape
    hp = npages // RING_SLOTS
    hq = N_HEADS * tn

    def key_copies(seq, c, parity):
        out = []
        if c == 0:
            for p in range(npages):
                pid = pt_ref[seq * npages + p]
                out.append(pltpu.make_async_copy(clft_hbm.at[layer, pid], lring.at[parity, p], sems.at[parity, 2]))
        for p in range(hp):
            pid = pt_ref[seq * npages + c * hp + p]
            out.append(pltpu.make_async_copy(ckt_hbm.at[layer, pid], kring.at[c, p], sems.at[c, 0]))
        return out

    def value_copies(seq, c):
        return [pltpu.make_async_copy(cvt_hbm.at[layer, pt_ref[seq * npages + c * hp + p]], vring.at[c, p],
                                      sems.at[c, 1]) for p in range(hp)]

    def start_fetch(seq, c, parity):
        for cp in key_copies(seq, c, parity) + value_copies(seq, c):
            cp.start(priority=PAGE_DMA_PRIORITY)

    @pl.when(i == 0)
    def _():
        for c in range(RING_SLOTS):
            start_fetch(0, c, 0)

    items = [(j, c) for j in range(sps) for c in range(RING_SLOTS)]
    n_ffn = -(-proj_refs[10].shape[1] // FFN_CHUNK)
    slots = [t * n_ffn // len(items) for t in range(len(items))]
    seq_state = {}
    scored = {}

    def score_phase(j, c):
        seq = i * sps + j
        parity = j % 2
        for cp in key_copies(seq, c, parity) + value_copies(seq, c):
            cp.wait()
        if c == 0:
            qbd, own_head, b_new, s_q, causal = _sample_setup(q_ref[j], lring.at[parity], lfn_ref[j], r_scr, npages)
            state = (jnp.full((hq, 1), -jnp.inf, F32), jnp.zeros((hq, 1), F32), jnp.zeros((hq, ATTN_WIDTH), F32))
            new = None
        else:
            qbd, own_head, b_new, s_q, causal, state = seq_state.pop(j)
            zrows = jnp.zeros((PAGE - tn, ATTN_WIDTH), F32)
            kn = jnp.concatenate([kn_ref[j], zrows], axis=0).astype(BF16)
            vn = jnp.concatenate([vn_ref[j], zrows], axis=0).astype(BF16)
            new = (kn, vn, b_new, causal)
        m = _sample_scores(qbd, s_q, kring.at[c], c * hp, r_scr, s_scr, state[0], new)
        scored[(j, c)] = (qbd, own_head, b_new, s_q, causal, state, m, new)

    def accumulate_phase(j, c):
        seq = i * sps + j
        qbd, own_head, b_new, s_q, causal, state, m, new = scored.pop((j, c))
        state = _sample_accumulate(vring.at[c], s_scr, state, m, new)
        start_fetch(jnp.minimum(seq + 1, nseq - 1), c, (j + 1) % 2)
        if c == 0:
            seq_state[j] = (qbd, own_head, b_new, s_q, causal, state)
        else:
            _, l, acc = state
            o = jnp.where(own_head, acc / l, 0.0)
            out = o[0:tn, :]
            for h in range(1, N_HEADS):
                out = out + o[h * tn:(h + 1) * tn, :]
            os_ref[j] = out

    def before_chunk(k):
        first = [t for t in range(len(items)) if slots[t] == k][:1]
        for t in first:
            score_phase(*items[t])

    def after_chunk(k):
        mine = [t for t in range(len(items)) if slots[t] == k]
        for n_t, t in enumerate(mine):
            if n_t > 0:
                score_phase(*items[t])
            accumulate_phase(*items[t])

    _outproj_math(*proj_refs, y_ref, final=final, before_chunk=before_chunk, after_chunk=after_chunk)

    @pl.when(i == pl.num_programs(0) - 1)
    def _():
        for c in range(RING_SLOTS):
            for cp in key_copies(nseq - 1, c, sps % 2) + value_copies(nseq - 1, c):
                cp.wait()


def _outproj_weights(w, gf):
    return [w['wa'], w['wb'], w['wc'], w['wo'], w['norm2_g'], w['wg'], w['wu'], w['wd'], gf]


def _outproj_attn(x2d, za, zb, oc, gates, w, gf, layer, page_table_flat, q, kn, vn, lfn, cache_kt, cache_vt,
                  cache_lft, *, final, tm, npages):
    n, d = x2d.shape
    nseq, tn, _ = q.shape
    steps = n // tm
    assert n % tm == 0 and nseq % steps == 0 and npages % (2 * RING_SLOTS) == 0
    sps = nseq // steps
    assert sps % 2 == 0
    hp = npages // RING_SLOTS
    row = lambda a: pl.BlockSpec((tm, a.shape[1]), lambda i, pt: (i, 0))
    const = lambda a: pl.BlockSpec(a.shape, lambda i, pt: (0,) * a.ndim, pipeline_mode=pl.Buffered(1))
    seqs = lambda width: pl.BlockSpec((sps, tn, width), lambda i, pt: (i, 0, 0))
    seqs_l = lambda width: pl.BlockSpec((None, sps, tn, width), lambda i, pt: (layer, i, 0, 0))
    hbm = pl.BlockSpec(memory_space=pl.ANY)
    weights = _outproj_weights(w, gf)
    grid_spec = pltpu.PrefetchScalarGridSpec(
        num_scalar_prefetch=1,
        grid=(steps,),
        in_specs=[row(x2d), row(za), row(zb), row(oc), row(gates)] + [const(a) for a in weights]
        + [seqs(ATTN_WIDTH), seqs_l(ATTN_WIDTH), seqs_l(ATTN_WIDTH),
           pl.BlockSpec((sps, N_HEADS, PAGE), lambda i, pt: (i, 0, 0)), hbm, hbm, hbm],
        out_specs=[pl.BlockSpec((tm, d), lambda i, pt: (i, 0)), seqs(ATTN_WIDTH)],
        scratch_shapes=[pltpu.VMEM((RING_SLOTS, hp, ATTN_WIDTH, PAGE), F32),
                        pltpu.VMEM((RING_SLOTS, hp, ATTN_WIDTH, PAGE), F32),
                        pltpu.VMEM((2, npages, N_HEADS, PAGE), F32),
                        pltpu.SemaphoreType.DMA((2, 3)),
                        pltpu.VMEM((N_HEADS * tn, (hp + 1) * PAGE), F32),
                        pltpu.VMEM(((npages + 1) * N_HEADS, PAGE), F32)],
    )
    return pl.pallas_call(
        functools.partial(_outproj_attn_body, final=final, layer=layer, npages=npages, nseq=nseq),
        grid_spec=grid_spec,
        out_shape=[jax.ShapeDtypeStruct((n, d), F32), jax.ShapeDtypeStruct((nseq, tn, ATTN_WIDTH), F32)],
        compiler_params=pltpu.CompilerParams(dimension_semantics=("arbitrary",), vmem_limit_bytes=VMEM_LIMIT),
        name="outproj_attn_final" if final else "outproj_attn",
    )(page_table_flat, x2d, za, zb, oc, gates, *weights, q, kn, vn, lfn, cache_kt, cache_vt, cache_lft)


def _outproj(x2d, za, zb, oc, gates, w, gf, *, final, tm):
    n, d = x2d.shape
    assert n % tm == 0
    row = lambda a: pl.BlockSpec((tm, a.shape[1]), lambda i: (i, 0))
    weights = _outproj_weights(w, gf)
    return pl.pallas_call(
        functools.partial(_outproj_body, final=final),
        grid=(n // tm,),
        in_specs=[row(x2d), row(za), row(zb), row(oc), row(gates)] + [_const_spec(a.shape) for a in weights],
        out_specs=pl.BlockSpec((tm, d), lambda i: (i, 0)),
        out_shape=jax.ShapeDtypeStruct((n, d), F32),
        compiler_params=pltpu.CompilerParams(dimension_semantics=("arbitrary",), vmem_limit_bytes=VMEM_LIMIT),
        name="outproj_final" if final else "outproj",
    )(x2d, za, zb, oc, gates, *weights)


def _layer_weights(l, norm1_g, w_in, b_f, conv_a_w, conv_b_w, conv_b_bias, cf_norm_g, cf_norm_b,
                   w_a_out, w_b_out, w_c_out, w_o, norm2_g, w_ffn_gate, w_ffn_up, w_ffn_down):
    d = w_in.shape[1]
    wi = w_in[l]
    off_f = 3 * ATTN_WIDTH
    off_mix = off_f + N_HEADS
    off_gate = off_mix + 3 * SC_WIDTH + 2 * CF_WIDTH
    wq, wk, wv = (wi[:, i * ATTN_WIDTH:(i + 1) * ATTN_WIDTH].astype(BF16) for i in range(3))
    pad = LANES - N_HEADS
    return {
        'norm1_g': norm1_g[l].reshape(1, d),
        'wq': wq, 'wk': wk, 'wv': wv, 'wq_t': wq.T, 'wk_t': wk.T, 'wv_t': wv.T,
        'wf': jnp.pad(wi[:, off_f:off_mix], ((0, 0), (0, pad))).astype(BF16),
        'bf': jnp.pad(b_f[l], (0, pad)).reshape(1, LANES),
        'wmix': wi[:, off_mix:off_gate].astype(BF16),
        'wgate': wi[:, off_gate:].astype(BF16),
        'conv_a_w': conv_a_w[l], 'conv_b_w': conv_b_w[l], 'conv_b_bias': conv_b_bias[l].reshape(1, -1),
        'cf_norm_g': cf_norm_g[l].reshape(1, -1), 'cf_norm_b': cf_norm_b[l].reshape(1, -1),
        'wa': w_a_out[l].astype(BF16), 'wb': w_b_out[l].astype(BF16), 'wc': w_c_out[l].astype(BF16),
        'wo': w_o[l].astype(BF16), 'norm2_g': norm2_g[l].reshape(1, d),
        'wg': w_ffn_gate[l].astype(BF16), 'wu': w_ffn_up[l].astype(BF16), 'wd': w_ffn_down[l].astype(BF16),
    }


def kernel(x_prompt, x_sample, cache_k, cache_v, cache_logf, state_conv_a, state_conv_b, page_table, norm1_g, w_in, b_f, conv_a_w, conv_b_w, conv_b_bias, cf_norm_g, cf_norm_b, w_a_out, w_b_out, w_c_out, w_o, norm2_g, w_ffn_gate, w_ffn_up, w_ffn_down, final_norm_g):
    depth = w_in.shape[0]
    pb, ps, d = x_prompt.shape
    sb_, st, _ = x_sample.shape
    n_pool = cache_k.shape[1]
    npages = page_table.shape[1]
    assert cache_k.shape[2:] == (PAGE, N_HEADS, HEAD_DIM) and ps % PAGE == 0
    cache_kt = cache_k.transpose(0, 1, 3, 4, 2).reshape(depth, n_pool, ATTN_WIDTH, PAGE)
    cache_vt = cache_v.transpose(0, 1, 3, 4, 2).reshape(depth, n_pool, ATTN_WIDTH, PAGE)
    cache_lft = cache_logf.transpose(0, 1, 3, 2)
    pt_flat = page_table.reshape(-1)
    gf = final_norm_g.reshape(1, d)
    tm_p = 512 if ps % 512 == 0 else PAGE
    tm_s = 512 if (sb_ * st) % 512 == 0 else sb_ * st
    tq = 256 if ps % 256 == 0 else PAGE
    bb = 16 if sb_ % 16 == 0 else sb_
    conv_rows = 64 if ps % 64 == 0 else ps

    xp = x_prompt.reshape(pb * ps, d)
    xs = x_sample.reshape(sb_ * st, d)
    outs = {k: [] for k in ('ap', 'bp', 'fs', 'as', 'bs')}
    prev_p = prev_s = None
    npp = ps // PAGE
    for l in range(depth):
        w = _layer_weights(l, norm1_g, w_in, b_f, conv_a_w, conv_b_w, conv_b_bias, cf_norm_g, cf_norm_b,
                           w_a_out, w_b_out, w_c_out, w_o, norm2_g, w_ffn_gate, w_ffn_up, w_ffn_down)
        final = l == depth - 1
        o_s = _inproj(xs, w, prev_s, paged=False, tm=tm_s)
        prev_s = (o_s['k'], o_s['v'])
        za_s, zb_s, na_s, nb_s = _mixers(o_s['ua'].reshape(sb_, st, -1), o_s['ub'].reshape(sb_, st, -1),
                                         o_s['sb'].reshape(sb_, st, -1), state_conv_a[l], state_conv_b[l], w,
                                         bb=bb, rows=st)
        lf = o_s['lf'][:, :N_HEADS].reshape(sb_, st, N_HEADS)
        lfn = jnp.pad(lf.transpose(0, 2, 1), ((0, 0), (0, 0), (0, PAGE - st)))
        o_p = _inproj(xp, w, prev_p, paged=True, tm=tm_p, tiles_per_seq=ps // tm_p, conv_rows=conv_rows)
        prev_p = (o_p['k'], o_p['v'], o_p['lf'])
        oc = _prompt_attn(o_p['q'], o_p['kb'], o_p['vb'], o_p['lf'], batch=pb, tq=tq)
        xp, oc_s = _outproj_attn(xp, o_p['za'].reshape(pb * ps, -1), o_p['zb'].reshape(pb * ps, -1), oc, o_p['gate'],
                                 w, gf, l, pt_flat, o_s['q'].reshape(sb_, st, -1),
                                 o_s['k'].reshape(l + 1, sb_, st, -1), o_s['v'].reshape(l + 1, sb_, st, -1),
                                 lfn, cache_kt, cache_vt, cache_lft, final=final, tm=tm_p, npages=npages)
        xs = _outproj(xs, za_s.reshape(sb_ * st, -1), zb_s.reshape(sb_ * st, -1), oc_s.reshape(sb_ * st, -1),
                      o_s['gate'], w, gf, final=final, tm=tm_s)
        outs['ap'].append(o_p['na'])
        outs['bp'].append(o_p['nb'])
        outs['fs'].append(lf)
        outs['as'].append(na_s)
        outs['bs'].append(nb_s)
    stack = lambda k: jnp.stack(outs[k])
    kt, vt, lft = prev_p
    to_pages = lambda a: a.reshape(depth, pb, npp, N_HEADS, HEAD_DIM, PAGE).transpose(0, 1, 2, 5, 3, 4)
    ks, vs = (a.reshape(depth, sb_, st, N_HEADS, HEAD_DIM) for a in prev_s)
    return (xp.reshape(pb, ps, d), xs.reshape(sb_, st, d), to_pages(kt), to_pages(vt),
            lft.reshape(depth, pb, npp, N_HEADS, PAGE).transpose(0, 1, 2, 4, 3), stack('ap'), stack('bp'),
            ks, vs, stack('fs'), stack('as'), stack('bs'))
```

```python
import functools

import jax
import jax.numpy as jnp
from jax import lax
from jax.experimental import pallas as pl
from jax.experimental.pallas import tpu as pltpu

F32 = jnp.float32
BF16 = jnp.bfloat16

EPS = 1e-6
LOG2E = 1.4426950408889634
N_HEADS = 8
HEAD_DIM = 64
ATTN_WIDTH = N_HEADS * HEAD_DIM
PAGE = 128
SC_WIDTH = 256
SC_KERNEL = 3
CF_WIDTH = 256
CF_KERNEL = 31
N_BRANCHES = 3
LANES = 128
VMEM_LIMIT = 56 * 1024 * 1024


def _nt(a, b):
    return lax.dot_general(a, b, (((1,), (1,)), ((), ())), preferred_element_type=F32)


def _nn(a, b):
    return jnp.dot(a, b, preferred_element_type=F32)


def _split3(x):
    hi = x.astype(BF16)
    r1 = x - hi.astype(F32)
    mid = r1.astype(BF16)
    lo = (r1 - mid.astype(F32)).astype(BF16)
    return hi, mid, lo


def _exact_nn(x, w01):
    hi, mid, lo = _split3(x)
    return _nn(hi, w01) + _nn(mid, w01) + _nn(lo, w01)


def _sigmoid(x):
    return 1.0 / (1.0 + jnp.exp(-x))


def _log_sigmoid(x):
    return jnp.minimum(x, 0.0) - jnp.log(1.0 + jnp.exp(-jnp.abs(x)))


def _rms(x, g):
    return x * lax.rsqrt(jnp.mean(x * x, axis=-1, keepdims=True) + EPS) * g


def _const_spec(shape):
    nd = len(shape)
    return pl.BlockSpec(shape, lambda *_: (0,) * nd, pipeline_mode=pl.Buffered(1))


def _inproj_body(r, *, paged, tiles_per_seq, conv_rows, casts):
    h = _rms(r['x'][...], r['g'][...]).astype(BF16)
    tm = h.shape[0]
    for name in casts:
        r[name + '_bf16'][...] = r[name + '_f32'][...].astype(BF16)
    last = r['k'].shape[0] - 1
    for j in range(last):
        r['k'][j] = r['k_prev'][j]
        r['v'][j] = r['v_prev'][j]
        if paged:
            r['lf'][j] = r['lf_prev'][j]

    def mix(c0, c1):
        return _nn(h, r['wmix'][:, c0:c1])

    gw = 512
    gate_cols = list(range(0, r['gate'].shape[1], gw))

    def gate(c):
        r['gate'][:, c:c + gw] = _sigmoid(_nn(h, r['wgate'][:, c:c + gw])).astype(BF16)

    f = _nn(h, r['wf'][...]) + r['bf'][...]
    lf = _log_sigmoid(f)
    sb = mix(0, SC_WIDTH)
    ua = mix(SC_WIDTH, 2 * SC_WIDTH) * mix(2 * SC_WIDTH, 3 * SC_WIDTH)
    c0 = 3 * SC_WIDTH
    ub = mix(c0, c0 + CF_WIDTH) * _sigmoid(mix(c0 + CF_WIDTH, c0 + 2 * CF_WIDTH))
    if paged:
        qt = (_nt(r['wq'][...], h) * (HEAD_DIM ** -0.5 * LOG2E)).astype(BF16)
        kt = _nt(r['wk'][...], h)
        vt = _nt(r['wv'][...], h)
        r['kb'][...] = kt.T.astype(BF16)
        lft = lf.T
        for p in range(tm // PAGE):
            sl = slice(p * PAGE, (p + 1) * PAGE)
            r['q'][p] = qt[:, sl]
            r['k'][last, p] = kt[:, sl]
            r['v'][last, p] = vt[:, sl]
            r['vb'][p] = vt[:, sl].astype(BF16)
            r['lf'][last, p] = lft[:N_HEADS, sl]
        sa, sbuf, sb_scr = r['sa'], r['sbuf'], r['sb_scr']
        a0 = A_PAD - (SC_KERNEL - 1)
        b0 = B_PAD - (CF_KERNEL - 1)
        i = pl.program_id(0)

        @pl.when(i == 0)
        def _():
            sa[...] = jnp.zeros(sa.shape, F32)
            sbuf[...] = jnp.zeros(sbuf.shape, F32)

        first = (i % tiles_per_seq) == 0
        sa[:, a0:A_PAD, :] = jnp.where(first, 0.0, sa[:, tm + a0:tm + A_PAD, :])
        sbuf[:, b0:B_PAD, :] = jnp.where(first, 0.0, sbuf[:, tm + b0:tm + B_PAD, :])
        sa[0, A_PAD:A_PAD + tm, :] = ua
        sbuf[0, B_PAD:B_PAD + tm, :] = ub
        sb_scr[0] = sb
        r['na'][...] = sa[:, tm + a0:tm + A_PAD, :]
        r['nb'][...] = sbuf[:, tm + b0:tm + B_PAD, :]
        n_conv = tm // conv_rows
        gate_after = {}
        for g in range(len(gate_cols)):
            gate_after.setdefault(g * n_conv // len(gate_cols), []).append(gate_cols[g])

        def after_chunk(k):
            for c in gate_after.get(k, []):
                gate(c)

        _conv_chunks(sa, sbuf, lambda r0: sb_scr[:, pl.ds(r0, conv_rows), :], r['conv_a_w'], r['conv_b_w'],
                     r['conv_b_bias'][...], r['cf_norm_g'][...], r['cf_norm_b'][...], r['za'], r['zb'],
                     t=tm, rows=conv_rows, unrolled=True, after_chunk=after_chunk)
    else:
        r['q'][...] = _nn(h, r['wq'][...]) * (HEAD_DIM ** -0.5)
        r['k'][last] = _nn(h, r['wk'][...])
        r['v'][last] = _nn(h, r['wv'][...])
        r['lf'][...] = lf
        r['sb'][...] = sb.astype(BF16)
        r['ua'][...] = ua
        r['ub'][...] = ub
        for c in gate_cols:
            gate(c)


def _inproj(x2d, w, prev, *, paged, tm, tiles_per_seq=1, conv_rows=64, casts=None):
    casts = casts or {}
    n, d = x2d.shape
    assert n % tm == 0 and tm % PAGE == 0
    steps = n // tm
    layers = 1 if prev is None else prev[0].shape[0] + 1
    row = lambda width, dt: (jax.ShapeDtypeStruct((n, width), dt), pl.BlockSpec((tm, width), lambda i: (i, 0)))
    row_l = lambda nl, width, dt: (jax.ShapeDtypeStruct((nl, n, width), dt),
                                   pl.BlockSpec((nl, tm, width), lambda i: (0, i, 0)))
    const = lambda a: (a, _const_spec(a.shape))
    ins = {'x': (x2d, pl.BlockSpec((tm, d), lambda i: (i, 0))), 'g': const(w['norm1_g']), 'wf': const(w['wf']),
           'bf': const(w['bf']), 'wmix': const(w['wmix']), 'wgate': const(w['wgate'])}
    scratch = {}
    if paged:
        npg = tm // PAGE
        pg = lambda rows, dt: (jax.ShapeDtypeStruct((n // PAGE, rows, PAGE), dt),
                               pl.BlockSpec((npg, rows, PAGE), lambda i: (i, 0, 0)))
        pg_l = lambda nl, rows, dt: (jax.ShapeDtypeStruct((nl, n // PAGE, rows, PAGE), dt),
                                     pl.BlockSpec((nl, npg, rows, PAGE), lambda i: (0, i, 0, 0)))
        tile = lambda width: (jax.ShapeDtypeStruct((steps, tm, width), BF16),
                              pl.BlockSpec((1, tm, width), lambda i: (i, 0, 0)))
        state = lambda rows, width: (jax.ShapeDtypeStruct((steps // tiles_per_seq, rows, width), F32),
                                     pl.BlockSpec((1, rows, width), lambda i: (i // tiles_per_seq, 0, 0)))
        for name in ('wq', 'wk', 'wv'):
            ins[name] = const(w[name + '_t'])
        for name in ('conv_a_w', 'conv_b_w', 'conv_b_bias', 'cf_norm_g', 'cf_norm_b'):
            ins[name] = const(w[name])
        outs = {'q': pg(ATTN_WIDTH, BF16), 'k': pg_l(layers, ATTN_WIDTH, F32), 'v': pg_l(layers, ATTN_WIDTH, F32),
                'kb': row(ATTN_WIDTH, BF16), 'vb': pg(ATTN_WIDTH, BF16), 'lf': pg_l(layers, N_HEADS, F32),
                'za': tile(SC_WIDTH), 'zb': tile(CF_WIDTH), 'na': state(SC_KERNEL - 1, SC_WIDTH),
                'nb': state(CF_KERNEL - 1, CF_WIDTH)}
        prev_specs = {'k_prev': pg_l(layers - 1, ATTN_WIDTH, F32), 'v_prev': pg_l(layers - 1, ATTN_WIDTH, F32),
                      'lf_prev': pg_l(layers - 1, N_HEADS, F32)}
        scratch = {'sa': pltpu.VMEM((1, tm + A_WIN, SC_WIDTH), F32), 'sbuf': pltpu.VMEM((1, tm + B_WIN, CF_WIDTH), F32),
                   'sb_scr': pltpu.VMEM((1, tm, SC_WIDTH), F32)}
    else:
        for name in ('wq', 'wk', 'wv'):
            ins[name] = const(w[name])
        outs = {'q': row(ATTN_WIDTH, F32), 'k': row_l(layers, ATTN_WIDTH, F32), 'v': row_l(layers, ATTN_WIDTH, F32),
                'lf': row(LANES, F32), 'sb': row(SC_WIDTH, BF16), 'ua': row(SC_WIDTH, F32), 'ub': row(CF_WIDTH, F32)}
        prev_specs = {'k_prev': row_l(layers - 1, ATTN_WIDTH, F32), 'v_prev': row_l(layers - 1, ATTN_WIDTH, F32)}
    outs['gate'] = row(N_BRANCHES * d, BF16)
    for name, a in casts.items():
        every = 1 if (a.shape[1] // steps) % 16 == 0 and a.shape[1] % steps == 0 else 2
        assert a.shape[1] % (steps // every) == 0 and (a.shape[1] * every // steps) % 16 == 0
        spec = pl.BlockSpec((a.shape[0], a.shape[1] * every // steps, a.shape[2]),
                            lambda i, every=every: (0, i // every, 0))
        ins[name + '_f32'] = (a, spec)
        outs[name + '_bf16'] = (jax.ShapeDtypeStruct(a.shape, BF16), spec)
    if prev is not None:
        for name, arr in zip(prev_specs, prev):
            ins[name] = (arr, prev_specs[name][1])
    names = list(ins) + list(outs) + list(scratch)

    def body(*refs):
        _inproj_body(dict(zip(names, refs)), paged=paged, tiles_per_seq=tiles_per_seq, conv_rows=conv_rows,
                     casts=list(casts))

    res = pl.pallas_call(
        body,
        grid=(steps,),
        in_specs=[v[1] for v in ins.values()],
        out_specs=[v[1] for v in outs.values()],
        out_shape=[v[0] for v in outs.values()],
        scratch_shapes=list(scratch.values()),
        compiler_params=pltpu.CompilerParams(dimension_semantics=("arbitrary",), vmem_limit_bytes=VMEM_LIMIT),
        name="inproj_paged" if paged else "inproj_rows",
    )(*[v[0] for v in ins.values()])
    return dict(zip(outs, res))


SUBLANES = 8
A_PAD = 8
B_PAD = 32
A_WIN = 8
B_WIN = 40


def _shift_rows(x, first, n):
    if first % SUBLANES == 0:
        return x[:, first:first + n, :]
    total = x.shape[1]
    return pltpu.roll(x, total - first, 1)[:, :n, :]


def _mixers_body(ua_ref, ub_ref, sb_ref, ha_ref, hb_ref, wa_ref, wb_ref, bias_ref, lng_ref, lnb_ref,
                 za_ref, zb_ref, na_ref, nb_ref, sa, sbuf, *, rows):
    bb, t, _ = ua_ref.shape
    a0 = A_PAD - (SC_KERNEL - 1)
    b0 = B_PAD - (CF_KERNEL - 1)
    sa[:, a0:A_PAD, :] = ha_ref[...]
    sa[:, A_PAD:A_PAD + t, :] = ua_ref[...]
    sbuf[:, b0:B_PAD, :] = hb_ref[...]
    sbuf[:, B_PAD:B_PAD + t, :] = ub_ref[...]
    sbuf[:, B_PAD + t:, :] = jnp.zeros((bb, B_WIN - B_PAD, CF_WIDTH), F32)
    sa[:, 0:a0, :] = jnp.zeros((bb, a0, SC_WIDTH), F32)
    sbuf[:, 0:b0, :] = jnp.zeros((bb, b0, CF_WIDTH), F32)
    na_ref[...] = sa[:, t + a0:t + A_PAD, :]
    nb_ref[...] = sbuf[:, t + b0:t + B_PAD, :]
    _conv_chunks(sa, sbuf, lambda r0: sb_ref[:, pl.ds(r0, rows), :].astype(F32), wa_ref, wb_ref, bias_ref[...],
                 lng_ref[...], lnb_ref[...], za_ref, zb_ref, t=t, rows=rows, unrolled=False)


def _conv_chunks(sa, sbuf, sb_at, wa_ref, wb_ref, bias, lng, lnb, za_ref, zb_ref, *, t, rows, unrolled,
                 after_chunk=None):
    a0 = A_PAD - (SC_KERNEL - 1)
    b0 = B_PAD - (CF_KERNEL - 1)

    def convs(r0):
        halves = []
        for c in range(0, CF_WIDTH, LANES):
            win = sbuf[:, pl.ds(r0, rows + B_WIN), c:c + LANES]
            acc = None
            for b in range(SUBLANES):
                n_a = (CF_KERNEL - 1 - b) // SUBLANES + 1
                shifted = _shift_rows(win, b0 + b, rows + SUBLANES * (n_a - 1))
                for a in range(n_a):
                    i = SUBLANES * a + b
                    term = wb_ref[i:i + 1, c:c + LANES] * shifted[:, SUBLANES * a:SUBLANES * a + rows, :]
                    acc = term if acc is None else acc + term
            halves.append(acc)
        win_a = sa[:, pl.ds(r0, rows + A_WIN), :]
        acc_a = wa_ref[0:1, :] * _shift_rows(win_a, a0, rows)
        for i in range(1, SC_KERNEL):
            acc_a = acc_a + wa_ref[i:i + 1, :] * _shift_rows(win_a, a0 + i, rows)
        za_ref[:, pl.ds(r0, rows), :] = (sb_at(r0) * acc_a).astype(BF16)
        return jnp.concatenate(halves, axis=-1)

    def norm_act(z, r0):
        z = z + bias
        mu = jnp.mean(z, axis=-1, keepdims=True)
        zc = z - mu
        var = jnp.mean(zc * zc, axis=-1, keepdims=True)
        y = zc * lax.rsqrt(var + EPS) * lng + lnb
        zb_ref[:, pl.ds(r0, rows), :] = (y * _sigmoid(y)).astype(BF16)

    if t == rows:
        norm_act(convs(0), 0)
    elif unrolled:
        z_prev = convs(0)
        for r0 in range(rows, t, rows):
            if after_chunk is not None:
                after_chunk(r0 // rows - 1)
            z = convs(r0)
            norm_act(z_prev, r0 - rows)
            z_prev = z
        norm_act(z_prev, t - rows)
        if after_chunk is not None:
            after_chunk(t // rows - 1)
    else:
        def step(c, z_prev):
            r0 = pl.multiple_of(c * rows, rows)
            z = convs(r0)
            norm_act(z_prev, pl.multiple_of((c - 1) * rows, rows))
            return z
        z_last = lax.fori_loop(1, t // rows, step, convs(0))
        norm_act(z_last, t - rows)


def _mixers(ua, ub, sb, hist_a, hist_b, w, *, bb, rows):
    b, t, _ = ua.shape
    assert b % bb == 0 and t % rows == 0
    seq = lambda width: pl.BlockSpec((bb, t, width), lambda i: (i, 0, 0))
    hist = lambda r, width: pl.BlockSpec((bb, r, width), lambda i: (i, 0, 0))
    consts = [w['conv_a_w'], w['conv_b_w'], w['conv_b_bias'], w['cf_norm_g'], w['cf_norm_b']]
    return pl.pallas_call(
        functools.partial(_mixers_body, rows=rows),
        grid=(b // bb,),
        in_specs=[seq(SC_WIDTH), seq(CF_WIDTH), seq(SC_WIDTH), hist(SC_KERNEL - 1, SC_WIDTH),
                  hist(CF_KERNEL - 1, CF_WIDTH)] + [_const_spec(a.shape) for a in consts],
        out_specs=[seq(SC_WIDTH), seq(CF_WIDTH), hist(SC_KERNEL - 1, SC_WIDTH), hist(CF_KERNEL - 1, CF_WIDTH)],
        out_shape=[jax.ShapeDtypeStruct((b, t, SC_WIDTH), BF16), jax.ShapeDtypeStruct((b, t, CF_WIDTH), BF16),
                   jax.ShapeDtypeStruct((b, SC_KERNEL - 1, SC_WIDTH), F32),
                   jax.ShapeDtypeStruct((b, CF_KERNEL - 1, CF_WIDTH), F32)],
        scratch_shapes=[pltpu.VMEM((bb, t + A_WIN, SC_WIDTH), F32), pltpu.VMEM((bb, t + B_WIN, CF_WIDTH), F32)],
        compiler_params=pltpu.CompilerParams(dimension_semantics=("arbitrary",), vmem_limit_bytes=VMEM_LIMIT),
        name=f"mixers_t{t}",
    )(ua, ub, sb, hist_a, hist_b, *consts)


def _tri(n, fn):
    r = lax.broadcasted_iota(jnp.int32, (n, n), 0)
    c = lax.broadcasted_iota(jnp.int32, (n, n), 1)
    return jnp.where(fn(r, c), 1.0, 0.0).astype(BF16)


def _prompt_attn_body(qt_ref, k_ref, vt_ref, lft_ref, o_ref, ct_scr, ccol_scr, crep_scr, m_scr, l_scr, acc_scr,
                      *, tq, heads_per_step):
    npages = qt_ref.shape[0]
    t = npages * PAGE
    ppb = tq // PAGE
    x = lft_ref[...].reshape(npages * N_HEADS, PAGE)
    cw = _exact_nn(x, _tri(PAGE, lambda r, c: r <= c))
    run = jnp.zeros((N_HEADS, 1), F32)
    zpad = jnp.zeros((LANES - N_HEADS, PAGE), F32)
    for p in range(npages):
        cp = cw[p * N_HEADS:(p + 1) * N_HEADS, :] + run
        run = cp[:, PAGE - 1:PAGE]
        cp2 = cp * LOG2E
        ct_scr[p] = cp2
        ccol_scr[p * PAGE:(p + 1) * PAGE, :] = jnp.concatenate([cp2, zpad], axis=0).T
    zq = jnp.zeros((HEAD_DIM, tq), BF16)
    key_i = lax.broadcasted_iota(jnp.int32, (tq, tq), 0)
    qry_i = lax.broadcasted_iota(jnp.int32, (tq, tq), 1)
    for h in range(N_HEADS):
        for r in range(0, t, tq):
            crep_scr[h, r:r + tq, :] = jnp.broadcast_to(ccol_scr[r:r + tq, h:h + 1], (tq, LANES))

    def paged(ref, blk, rws):
        return jnp.concatenate([ref[blk * ppb + j, rws, :] for j in range(ppb)], axis=1)

    for h0 in range(0, N_HEADS, heads_per_step):
        heads = range(h0, h0 + heads_per_step)

        def q_block(qi, carry):
            q0 = pl.multiple_of(qi * tq, tq)
            for h in heads:
                m_scr[h] = jnp.full((1, tq), -jnp.inf, F32)
                l_scr[h] = jnp.zeros((1, tq), F32)
                acc_scr[h] = jnp.zeros((HEAD_DIM, tq), F32)

            def k_step(blocks):
                s, pe, alpha = {}, {}, {}

                def scores(kj, masked, h):
                    k0 = pl.multiple_of(kj * tq, tq)
                    pair = slice((h // 2) * 2 * HEAD_DIM, (h // 2 + 1) * 2 * HEAD_DIM)
                    qh = paged(qt_ref, qi, slice(h * HEAD_DIM, (h + 1) * HEAD_DIM))
                    qtz = jnp.concatenate([qh, zq] if h % 2 == 0 else [zq, qh], axis=0)
                    cs = crep_scr[h, pl.ds(k0, tq), :]
                    sh = _nn(k_ref[pl.ds(k0, tq), pair], qtz) - jnp.concatenate([cs] * (tq // LANES), axis=1)
                    return jnp.where(key_i <= qry_i, sh, -jnp.inf) if masked else sh

                def softmax(sh, h):
                    ct = jnp.concatenate([ct_scr[qi * ppb + j, h:h + 1, :] for j in range(ppb)], axis=1)
                    m = m_scr[h]
                    m_new = jnp.maximum(m, jnp.max(sh, axis=0, keepdims=True) + ct)
                    a = jnp.exp2(m - m_new)
                    p = jnp.exp2(sh + (ct - m_new))
                    m_scr[h] = m_new
                    l_scr[h] = a * l_scr[h] + jnp.sum(p, axis=0, keepdims=True)
                    return a, p.astype(BF16)

                def weighted_values(kj, h, a, p):
                    vt = paged(vt_ref, kj, slice(h * HEAD_DIM, (h + 1) * HEAD_DIM))
                    acc_scr[h] = acc_scr[h] * a + _nn(vt, p)

                todo = [(kj, masked, h) for kj, masked in blocks for h in heads]
                for i in range(len(todo) + 2):
                    if i < len(todo):
                        s[i] = scores(*todo[i])
                    if 0 <= i - 2 < len(todo):
                        weighted_values(todo[i - 2][0], todo[i - 2][2], alpha.pop(i - 2), pe.pop(i - 2))
                    if 0 <= i - 1 < len(todo):
                        alpha[i - 1], pe[i - 1] = softmax(s.pop(i - 1), todo[i - 1][2])

            def pair_step(kp, c):
                k_step([(2 * kp, False), (2 * kp + 1, False)])
                return c

            def single_step(kj, c):
                k_step([(kj, False)])
                return c

            lax.fori_loop(0, qi // 2, pair_step, 0)
            lax.fori_loop(2 * (qi // 2), qi, single_step, 0)
            k_step([(qi, True)])
            for hp in range(h0 // 2, (h0 + heads_per_step) // 2):
                ot = jnp.concatenate([acc_scr[2 * hp + e] / l_scr[2 * hp + e] for e in range(2)], axis=0)
                o_ref[pl.ds(q0, tq), hp * 2 * HEAD_DIM:(hp + 1) * 2 * HEAD_DIM] = ot.T.astype(o_ref.dtype)
            return carry

        lax.fori_loop(0, t // tq, q_block, 0)


ATTN_HEADS_PER_STEP = 8


def _prompt_attn(qt, kb, vtb, lft_layers, *, batch, tq):
    n, _ = kb.shape
    t = n // batch
    npages = t // PAGE
    layer = lft_layers.shape[0] - 1
    assert t % tq == 0 and tq % PAGE == 0
    paged_spec = pl.BlockSpec((npages, ATTN_WIDTH, PAGE), lambda b: (b, 0, 0))
    return pl.pallas_call(
        functools.partial(_prompt_attn_body, tq=tq, heads_per_step=ATTN_HEADS_PER_STEP),
        grid=(batch,),
        in_specs=[paged_spec, pl.BlockSpec((t, ATTN_WIDTH), lambda b: (b, 0)), paged_spec,
                  pl.BlockSpec((None, npages, N_HEADS, PAGE), lambda b: (layer, b, 0, 0))],
        out_specs=pl.BlockSpec((t, ATTN_WIDTH), lambda b: (b, 0)),
        out_shape=jax.ShapeDtypeStruct((n, ATTN_WIDTH), BF16),
        scratch_shapes=[pltpu.VMEM((npages, N_HEADS, PAGE), F32), pltpu.VMEM((t, LANES), F32),
                        pltpu.VMEM((N_HEADS, t, LANES), F32), pltpu.VMEM((N_HEADS, 1, tq), F32),
                        pltpu.VMEM((N_HEADS, 1, tq), F32), pltpu.VMEM((N_HEADS, HEAD_DIM, tq), F32)],
        compiler_params=pltpu.CompilerParams(dimension_semantics=("arbitrary",), vmem_limit_bytes=VMEM_LIMIT),
        name="prompt_attn",
    )(qt, kb, vtb, lft_layers)


def _bias_rows(r_scr, blk, tn):
    return jnp.concatenate(
        [jnp.broadcast_to(r_scr[blk * N_HEADS + h:blk * N_HEADS + h + 1, :], (tn, PAGE)) for h in range(N_HEADS)],
        axis=0)


def _sample_setup(q, lft_ref, lfn, r_scr, npages):
    tn = q.shape[0]
    hq = N_HEADS * tn
    row = lax.broadcasted_iota(jnp.int32, (hq, ATTN_WIDTH), 0)
    lane = lax.broadcasted_iota(jnp.int32, (hq, ATTN_WIDTH), 1)
    own_head = (lane // HEAD_DIM) == (row // tn)
    qbd = jnp.where(own_head, jnp.concatenate([q] * N_HEADS, axis=0), 0.0).astype(BF16)
    nblk = npages + 1
    x = jnp.concatenate([lft_ref[p] for p in range(npages)] + [lfn], axis=0)
    rw = _exact_nn(x, _tri(PAGE, lambda r, c: r > c))
    tot = rw[:, 0:1] + x[:, 0:1]
    run = jnp.zeros((N_HEADS, 1), F32)
    for p in reversed(range(nblk)):
        hs = slice(p * N_HEADS, (p + 1) * N_HEADS)
        r_scr[hs, :] = rw[hs, :] + run
        run = run + tot[hs, :]
    row_p = lax.broadcasted_iota(jnp.int32, (hq, PAGE), 0) % tn
    lane_p = lax.broadcasted_iota(jnp.int32, (hq, PAGE), 1)
    b_new = _bias_rows(r_scr, npages, tn)
    s_q = jnp.sum(jnp.where(lane_p == row_p, b_new, 0.0), axis=-1, keepdims=True)
    return qbd, own_head, b_new, s_q, lane_p <= row_p


def _sample_scores(qbd, s_q, k_pages, page0, r_scr, s_scr, m_prev, new):
    hq = qbd.shape[0]
    tn = hq // N_HEADS
    hp = k_pages.shape[0]
    m = m_prev
    for pp in range(0, hp, 2):
        kt2 = jnp.concatenate([k_pages[pp], k_pages[pp + 1]], axis=1).astype(BF16)
        bias = jnp.concatenate([_bias_rows(r_scr, page0 + pp, tn), _bias_rows(r_scr, page0 + pp + 1, tn)], axis=1)
        s = _nn(qbd, kt2) + bias - s_q
        s_scr[:, pp * PAGE:(pp + 2) * PAGE] = s
        m = jnp.maximum(m, jnp.max(s, axis=-1, keepdims=True))
    if new is not None:
        kn, vn, b_new, causal = new
        s = jnp.where(causal, _nt(qbd, kn) + b_new - s_q, -jnp.inf)
        s_scr[:, hp * PAGE:(hp + 1) * PAGE] = s
        m = jnp.maximum(m, jnp.max(s, axis=-1, keepdims=True))
    return m


def _sample_accumulate(v_pages, s_scr, state, m, new):
    m_prev, l_prev, acc_prev = state
    hp = v_pages.shape[0]
    alpha = jnp.exp(m_prev - m)
    l = alpha * l_prev
    acc = alpha * acc_prev
    for pp in range(0, hp, 2):
        pe = jnp.exp(s_scr[:, pp * PAGE:(pp + 2) * PAGE] - m)
        l = l + jnp.sum(pe, axis=-1, keepdims=True)
        vt2 = jnp.concatenate([v_pages[pp], v_pages[pp + 1]], axis=1).astype(BF16)
        acc = acc + _nt(pe.astype(BF16), vt2)
    if new is not None:
        vn = new[1]
        pe = jnp.exp(s_scr[:, hp * PAGE:(hp + 1) * PAGE] - m)
        l = l + jnp.sum(pe, axis=-1, keepdims=True)
        acc = acc + _nn(pe.astype(BF16), vn)
    return m, l, acc


FFN_CHUNK = 256
SEQ_HALVES = 2
RING_SLOTS = 3
PAGE_DMA_PRIORITY = 1


def _outproj_math(x_ref, za_ref, zb_ref, oc_ref, gate_ref, wa_ref, wb_ref, wc_ref, wo_ref, g2_ref,
                  wg_ref, wu_ref, wd_ref, gf_ref, y_ref, *, final, before_chunk=None, after_chunk=None):
    d = x_ref.shape[1]
    ya = _nn(za_ref[...].astype(BF16), wa_ref[...])
    yb = _nn(zb_ref[...].astype(BF16), wb_ref[...])
    yc = _nn(oc_ref[...].astype(BF16), wc_ref[...])
    merged = (gate_ref[:, 0:d].astype(F32) * ya + gate_ref[:, d:2 * d].astype(F32) * yb
              + gate_ref[:, 2 * d:3 * d].astype(F32) * yc)
    x1 = x_ref[...] + _nn(merged.astype(BF16), wo_ref[...])
    h2 = _rms(x1, g2_ref[...]).astype(BF16)
    acc = x1
    for k, c in enumerate(range(0, wg_ref.shape[1], FFN_CHUNK)):
        if before_chunk is not None:
            before_chunk(k)
        gch = _nn(h2, wg_ref[:, c:c + FFN_CHUNK])
        uch = _nn(h2, wu_ref[:, c:c + FFN_CHUNK])
        acc = acc + _nn((gch * _sigmoid(gch) * uch).astype(BF16), wd_ref[c:c + FFN_CHUNK, :])
        if after_chunk is not None:
            after_chunk(k)
    y_ref[...] = _rms(acc, gf_ref[...]) if final else acc


def _outproj_body(*refs, final):
    _outproj_math(*refs, final=final)


def _outproj_attn_body(pt_ref, *refs, final, layer, npages, nseq):
    proj_refs = refs[:14]
    q_ref, kn_ref, vn_ref, lfn_ref, ckt_hbm, cvt_hbm, clft_hbm, y_ref, os_ref = refs[14:23]
    kring, vring, lring, sems, s_scr, r_scr = refs[23:]
    i = pl.program_id(0)
    sps, tn, _ = q_ref.shape
    hp = npages // SEQ_HALVES
    hq = N_HEADS * tn

    def slot_of(seq, c):
        return (seq * SEQ_HALVES + c) % RING_SLOTS

    def copies(seq, c, slot, parity):
        out = []
        if c == 0:
            for p in range(npages):
                pid = pt_ref[seq * npages + p]
                out.append(pltpu.make_async_copy(clft_hbm.at[layer, pid], lring.at[parity, p], sems.at[parity, 2]))
        for p in range(hp):
            pid = pt_ref[seq * npages + c * hp + p]
            out.append(pltpu.make_async_copy(ckt_hbm.at[layer, pid], kring.at[slot, p], sems.at[slot, 0]))
        for p in range(hp):
            pid = pt_ref[seq * npages + c * hp + p]
            out.append(pltpu.make_async_copy(cvt_hbm.at[layer, pid], vring.at[slot, p], sems.at[slot, 1]))
        return out

    def fetch_after(j, c):
        seq = i * sps + j
        ahead = c + RING_SLOTS
        tgt_seq = jnp.minimum(seq + ahead // SEQ_HALVES, nseq - 1)
        return copies(tgt_seq, ahead % SEQ_HALVES, slot_of(seq, c), (j + ahead // SEQ_HALVES) % 2)

    @pl.when(i == 0)
    def _():
        for g in range(RING_SLOTS):
            for cp in copies(g // SEQ_HALVES, g % SEQ_HALVES, g, (g // SEQ_HALVES) % 2):
                cp.start(priority=PAGE_DMA_PRIORITY)

    items = [(j, c) for j in range(sps) for c in range(SEQ_HALVES)]
    n_ffn = -(-proj_refs[10].shape[1] // FFN_CHUNK)
    slots = [t * n_ffn // len(items) for t in range(len(items))]
    seq_state = {}
    scored = {}

    def score_phase(j, c):
        seq = i * sps + j
        parity = j % 2
        slot = slot_of(seq, c)
        for cp in copies(seq, c, slot, parity):
            cp.wait()
        if c == 0:
            qbd, own_head, b_new, s_q, causal = _sample_setup(q_ref[j], lring.at[parity], lfn_ref[j], r_scr, npages)
            state = (jnp.full((hq, 1), -jnp.inf, F32), jnp.zeros((hq, 1), F32), jnp.zeros((hq, ATTN_WIDTH), F32))
            new = None
        else:
            qbd, own_head, b_new, s_q, causal, state = seq_state.pop(j)
            zrows = jnp.zeros((PAGE - tn, ATTN_WIDTH), F32)
            kn = jnp.concatenate([kn_ref[j], zrows], axis=0).astype(BF16)
            vn = jnp.concatenate([vn_ref[j], zrows], axis=0).astype(BF16)
            new = (kn, vn, b_new, causal)
        m = _sample_scores(qbd, s_q, kring.at[slot], c * hp, r_scr, s_scr, state[0], new)
        scored[(j, c)] = (qbd, own_head, b_new, s_q, causal, state, m, new, slot)

    def accumulate_phase(j, c):
        seq = i * sps + j
        qbd, own_head, b_new, s_q, causal, state, m, new, slot = scored.pop((j, c))
        state = _sample_accumulate(vring.at[slot], s_scr, state, m, new)
        for cp in fetch_after(j, c):
            cp.start(priority=PAGE_DMA_PRIORITY)
        if c == 0:
            seq_state[j] = (qbd, own_head, b_new, s_q, causal, state)
        else:
            _, l, acc = state
            o = jnp.where(own_head, acc / l, 0.0)
            out = o[0:tn, :]
            for h in range(1, N_HEADS):
                out = out + o[h * tn:(h + 1) * tn, :]
            os_ref[j] = out

    def before_chunk(k):
        first = [t for t in range(len(items)) if slots[t] == k][:1]
        for t in first:
            score_phase(*items[t])

    def after_chunk(k):
        mine = [t for t in range(len(items)) if slots[t] == k]
        for n_t, t in enumerate(mine):
            if n_t > 0:
                score_phase(*items[t])
            accumulate_phase(*items[t])

    _outproj_math(*proj_refs, y_ref, final=final, before_chunk=before_chunk, after_chunk=after_chunk)

    @pl.when(i == pl.num_programs(0) - 1)
    def _():
        for j, c in items[-RING_SLOTS:]:
            for cp in fetch_after(j, c):
                cp.wait()


def _outproj_weights(w, gf, layer):
    const = lambda a: (a, pl.BlockSpec(a.shape, lambda *_: (0,) * a.ndim, pipeline_mode=pl.Buffered(1)))
    of_layer = lambda a: (a, pl.BlockSpec((None,) + a.shape[1:], lambda *_: (layer,) + (0,) * (a.ndim - 1),
                                          pipeline_mode=pl.Buffered(1)))
    return [const(w['wa']), const(w['wb']), const(w['wc']), const(w['wo']), const(w['norm2_g']),
            of_layer(w['wg_layers']), of_layer(w['wu_layers']), of_layer(w['wd_layers']), const(gf)]


def _outproj_attn(x2d, za, zb, oc, gates, w, gf, layer, page_table_flat, q, kn, vn, lfn, cache_kt, cache_vt,
                  cache_lft, *, final, tm, npages):
    n, d = x2d.shape
    nseq, tn, _ = q.shape
    steps = n // tm
    assert n % tm == 0 and nseq % steps == 0 and npages % (2 * SEQ_HALVES) == 0
    sps = nseq // steps
    assert sps % 2 == 0
    hp = npages // SEQ_HALVES
    assert nseq * SEQ_HALVES >= RING_SLOTS
    row = lambda a: pl.BlockSpec((tm, a.shape[1]), lambda i, pt: (i, 0))
    seqs = lambda width: pl.BlockSpec((sps, tn, width), lambda i, pt: (i, 0, 0))
    seqs_l = lambda width: pl.BlockSpec((None, sps, tn, width), lambda i, pt: (layer, i, 0, 0))
    hbm = pl.BlockSpec(memory_space=pl.ANY)
    weights = _outproj_weights(w, gf, layer)
    grid_spec = pltpu.PrefetchScalarGridSpec(
        num_scalar_prefetch=1,
        grid=(steps,),
        in_specs=[row(x2d), row(za), row(zb), row(oc), row(gates)] + [s for _, s in weights]
        + [seqs(ATTN_WIDTH), seqs_l(ATTN_WIDTH), seqs_l(ATTN_WIDTH),
           pl.BlockSpec((sps, N_HEADS, PAGE), lambda i, pt: (i, 0, 0)), hbm, hbm, hbm],
        out_specs=[pl.BlockSpec((tm, d), lambda i, pt: (i, 0)), seqs(ATTN_WIDTH)],
        scratch_shapes=[pltpu.VMEM((RING_SLOTS, hp, ATTN_WIDTH, PAGE), F32),
                        pltpu.VMEM((RING_SLOTS, hp, ATTN_WIDTH, PAGE), F32),
                        pltpu.VMEM((2, npages, N_HEADS, PAGE), F32),
                        pltpu.SemaphoreType.DMA((RING_SLOTS, 3)),
                        pltpu.VMEM((N_HEADS * tn, (hp + 1) * PAGE), F32),
                        pltpu.VMEM(((npages + 1) * N_HEADS, PAGE), F32)],
    )
    return pl.pallas_call(
        functools.partial(_outproj_attn_body, final=final, layer=layer, npages=npages, nseq=nseq),
        grid_spec=grid_spec,
        out_shape=[jax.ShapeDtypeStruct((n, d), F32), jax.ShapeDtypeStruct((nseq, tn, ATTN_WIDTH), F32)],
        compiler_params=pltpu.CompilerParams(dimension_semantics=("arbitrary",), vmem_limit_bytes=VMEM_LIMIT),
        name="outproj_attn_final" if final else "outproj_attn",
    )(page_table_flat, x2d, za, zb, oc, gates, *[a for a, _ in weights], q, kn, vn, lfn, cache_kt, cache_vt, cache_lft)


def _outproj(x2d, za, zb, oc, gates, w, gf, layer, *, final, tm):
    n, d = x2d.shape
    assert n % tm == 0
    row = lambda a: pl.BlockSpec((tm, a.shape[1]), lambda i: (i, 0))
    weights = _outproj_weights(w, gf, layer)
    return pl.pallas_call(
        functools.partial(_outproj_body, final=final),
        grid=(n // tm,),
        in_specs=[row(x2d), row(za), row(zb), row(oc), row(gates)] + [s for _, s in weights],
        out_specs=pl.BlockSpec((tm, d), lambda i: (i, 0)),
        out_shape=jax.ShapeDtypeStruct((n, d), F32),
        compiler_params=pltpu.CompilerParams(dimension_semantics=("arbitrary",), vmem_limit_bytes=VMEM_LIMIT),
        name="outproj_final" if final else "outproj",
    )(x2d, za, zb, oc, gates, *[a for a, _ in weights])


def _layer_weights(l, norm1_g, w_in, b_f, conv_a_w, conv_b_w, conv_b_bias, cf_norm_g, cf_norm_b,
                   w_a_out, w_b_out, w_c_out, w_o, norm2_g, w_ffn_gate, w_ffn_up, w_ffn_down):
    d = w_in.shape[1]
    wi = w_in[l]
    off_f = 3 * ATTN_WIDTH
    off_mix = off_f + N_HEADS
    off_gate = off_mix + 3 * SC_WIDTH + 2 * CF_WIDTH
    wq, wk, wv = (wi[:, i * ATTN_WIDTH:(i + 1) * ATTN_WIDTH].astype(BF16) for i in range(3))
    pad = LANES - N_HEADS
    return {
        'norm1_g': norm1_g[l].reshape(1, d),
        'wq': wq, 'wk': wk, 'wv': wv, 'wq_t': wq.T, 'wk_t': wk.T, 'wv_t': wv.T,
        'wf': jnp.pad(wi[:, off_f:off_mix], ((0, 0), (0, pad))).astype(BF16),
        'bf': jnp.pad(b_f[l], (0, pad)).reshape(1, LANES),
        'wmix': wi[:, off_mix:off_gate].astype(BF16),
        'wgate': wi[:, off_gate:].astype(BF16),
        'conv_a_w': conv_a_w[l], 'conv_b_w': conv_b_w[l], 'conv_b_bias': conv_b_bias[l].reshape(1, -1),
        'cf_norm_g': cf_norm_g[l].reshape(1, -1), 'cf_norm_b': cf_norm_b[l].reshape(1, -1),
        'wa': w_a_out[l].astype(BF16), 'wb': w_b_out[l].astype(BF16), 'wc': w_c_out[l].astype(BF16),
        'wo': w_o[l].astype(BF16), 'norm2_g': norm2_g[l].reshape(1, d),
    }


def kernel(x_prompt, x_sample, cache_k, cache_v, cache_logf, state_conv_a, state_conv_b, page_table, norm1_g, w_in, b_f, conv_a_w, conv_b_w, conv_b_bias, cf_norm_g, cf_norm_b, w_a_out, w_b_out, w_c_out, w_o, norm2_g, w_ffn_gate, w_ffn_up, w_ffn_down, final_norm_g):
    depth = w_in.shape[0]
    pb, ps, d = x_prompt.shape
    sb_, st, _ = x_sample.shape
    n_pool = cache_k.shape[1]
    npages = page_table.shape[1]
    assert cache_k.shape[2:] == (PAGE, N_HEADS, HEAD_DIM) and ps % PAGE == 0
    cache_kt = cache_k.transpose(0, 1, 3, 4, 2).reshape(depth, n_pool, ATTN_WIDTH, PAGE)
    cache_vt = cache_v.transpose(0, 1, 3, 4, 2).reshape(depth, n_pool, ATTN_WIDTH, PAGE)
    cache_lft = cache_logf.transpose(0, 1, 3, 2)
    pt_flat = page_table.reshape(-1)
    gf = final_norm_g.reshape(1, d)
    tm_p = 512 if ps % 512 == 0 else PAGE
    tm_s = 512 if (sb_ * st) % 512 == 0 else sb_ * st
    tq = 256 if ps % 256 == 0 else PAGE
    bb = 16 if sb_ % 16 == 0 else sb_
    conv_rows = 64 if ps % 64 == 0 else ps

    xp = x_prompt.reshape(pb * ps, d)
    xs = x_sample.reshape(sb_ * st, d)
    outs = {k: [] for k in ('ap', 'bp', 'fs', 'as', 'bs')}
    prev_p = prev_s = None
    npp = ps // PAGE
    for l in range(depth):
        w = _layer_weights(l, norm1_g, w_in, b_f, conv_a_w, conv_b_w, conv_b_bias, cf_norm_g, cf_norm_b,
                           w_a_out, w_b_out, w_c_out, w_o, norm2_g, w_ffn_gate, w_ffn_up, w_ffn_down)
        final = l == depth - 1
        o_s = _inproj(xs, w, prev_s, paged=False, tm=tm_s)
        prev_s = (o_s['k'], o_s['v'])
        za_s, zb_s, na_s, nb_s = _mixers(o_s['ua'].reshape(sb_, st, -1), o_s['ub'].reshape(sb_, st, -1),
                                         o_s['sb'].reshape(sb_, st, -1), state_conv_a[l], state_conv_b[l], w,
                                         bb=bb, rows=st)
        lf = o_s['lf'][:, :N_HEADS].reshape(sb_, st, N_HEADS)
        lfn = jnp.pad(lf.transpose(0, 2, 1), ((0, 0), (0, 0), (0, PAGE - st)))
        casts = {'wg': w_ffn_gate, 'wu': w_ffn_up, 'wd': w_ffn_down} if l == 0 else None
        o_p = _inproj(xp, w, prev_p, paged=True, tm=tm_p, tiles_per_seq=ps // tm_p, conv_rows=conv_rows, casts=casts)
        if l == 0:
            ffn = {name + '_layers': o_p[name + '_bf16'] for name in casts}
        w.update(ffn)
        prev_p = (o_p['k'], o_p['v'], o_p['lf'])
        oc = _prompt_attn(o_p['q'], o_p['kb'], o_p['vb'], o_p['lf'], batch=pb, tq=tq)
        xp, oc_s = _outproj_attn(xp, o_p['za'].reshape(pb * ps, -1), o_p['zb'].reshape(pb * ps, -1), oc, o_p['gate'],
                                 w, gf, l, pt_flat, o_s['q'].reshape(sb_, st, -1),
                                 o_s['k'].reshape(l + 1, sb_, st, -1), o_s['v'].reshape(l + 1, sb_, st, -1),
                                 lfn, cache_kt, cache_vt, cache_lft, final=final, tm=tm_p, npages=npages)
        xs = _outproj(xs, za_s.reshape(sb_ * st, -1), zb_s.reshape(sb_ * st, -1), oc_s.reshape(sb_ * st, -1),
                      o_s['gate'], w, gf, l, final=final, tm=tm_s)
        outs['ap'].append(o_p['na'])
        outs['bp'].append(o_p['nb'])
        outs['fs'].append(lf)
        outs['as'].append(na_s)
        outs['bs'].append(nb_s)
    stack = lambda k: jnp.stack(outs[k])
    kt, vt, lft = prev_p
    to_pages = lambda a: a.reshape(depth, pb, npp, N_HEADS, HEAD_DIM, PAGE).transpose(0, 1, 2, 5, 3, 4)
    ks, vs = (a.reshape(depth, sb_, st, N_HEADS, HEAD_DIM) for a in prev_s)
    return (xp.reshape(pb, ps, d), xs.reshape(sb_, st, d), to_pages(kt), to_pages(vt),
            lft.reshape(depth, pb, npp, N_HEADS, PAGE).transpose(0, 1, 2, 4, 3), stack('ap'), stack('bp'),
            ks, vs, stack('fs'), stack('as'), stack('bs'))
```

```python
import functools

import jax
import jax.numpy as jnp
from jax import lax
from jax.experimental import pallas as pl
from jax.experimental.pallas import tpu as pltpu

F32 = jnp.float32
BF16 = jnp.bfloat16

EPS = 1e-6
LOG2E = 1.4426950408889634
N_HEADS = 8
HEAD_DIM = 64
ATTN_WIDTH = N_HEADS * HEAD_DIM
PAGE = 128
SC_WIDTH = 256
SC_KERNEL = 3
CF_WIDTH = 256
CF_KERNEL = 31
N_BRANCHES = 3
LANES = 128
VMEM_LIMIT = 56 * 1024 * 1024


def _nt(a, b):
    return lax.dot_general(a, b, (((1,), (1,)), ((), ())), preferred_element_type=F32)


def _nn(a, b):
    return jnp.dot(a, b, preferred_element_type=F32)


def _split3(x):
    hi = x.astype(BF16)
    r1 = x - hi.astype(F32)
    mid = r1.astype(BF16)
    lo = (r1 - mid.astype(F32)).astype(BF16)
    return hi, mid, lo


def _exact_nn(x, w01):
    hi, mid, lo = _split3(x)
    return _nn(hi, w01) + _nn(mid, w01) + _nn(lo, w01)


def _sigmoid(x):
    return 1.0 / (1.0 + jnp.exp(-x))


def _log_sigmoid(x):
    return jnp.minimum(x, 0.0) - jnp.log(1.0 + jnp.exp(-jnp.abs(x)))


def _rms(x, g):
    return x * lax.rsqrt(jnp.mean(x * x, axis=-1, keepdims=True) + EPS) * g


def _const_spec(shape):
    nd = len(shape)
    return pl.BlockSpec(shape, lambda *_: (0,) * nd, pipeline_mode=pl.Buffered(1))


def _inproj_body(r, *, paged, tiles_per_seq, conv_rows, casts):
    h = _rms(r['x'][...], r['g'][...]).astype(BF16)
    tm = h.shape[0]
    for name in casts:
        r[name + '_bf16'][...] = r[name + '_f32'][...].astype(BF16)
    last = r['k'].shape[0] - 1
    for j in range(last):
        r['k'][j] = r['k_prev'][j]
        r['v'][j] = r['v_prev'][j]
        if paged:
            r['lf'][j] = r['lf_prev'][j]

    def mix(c0, c1):
        return _nn(h, r['wmix'][:, c0:c1])

    gw = 512
    gate_cols = list(range(0, r['gate'].shape[1], gw))

    def gate(c):
        r['gate'][:, c:c + gw] = _sigmoid(_nn(h, r['wgate'][:, c:c + gw])).astype(BF16)

    f = _nn(h, r['wf'][...]) + r['bf'][...]
    lf = _log_sigmoid(f)
    sb = mix(0, SC_WIDTH)
    ua = mix(SC_WIDTH, 2 * SC_WIDTH) * mix(2 * SC_WIDTH, 3 * SC_WIDTH)
    c0 = 3 * SC_WIDTH
    ub = mix(c0, c0 + CF_WIDTH) * _sigmoid(mix(c0 + CF_WIDTH, c0 + 2 * CF_WIDTH))
    if paged:
        qt = (_nt(r['wq'][...], h) * (HEAD_DIM ** -0.5 * LOG2E)).astype(BF16)
        kt = _nt(r['wk'][...], h)
        vt = _nt(r['wv'][...], h)
        r['kb'][...] = kt.T.astype(BF16)
        lft = lf.T
        for p in range(tm // PAGE):
            sl = slice(p * PAGE, (p + 1) * PAGE)
            r['q'][p] = qt[:, sl]
            r['k'][last, p] = kt[:, sl]
            r['v'][last, p] = vt[:, sl]
            r['vb'][p] = vt[:, sl].astype(BF16)
            r['lf'][last, p] = lft[:N_HEADS, sl]
        sa, sbuf, sb_scr = r['sa'], r['sbuf'], r['sb_scr']
        a0 = A_PAD - (SC_KERNEL - 1)
        b0 = B_PAD - (CF_KERNEL - 1)
        i = pl.program_id(0)

        @pl.when(i == 0)
        def _():
            sa[...] = jnp.zeros(sa.shape, F32)
            sbuf[...] = jnp.zeros(sbuf.shape, F32)

        first = (i % tiles_per_seq) == 0
        sa[:, a0:A_PAD, :] = jnp.where(first, 0.0, sa[:, tm + a0:tm + A_PAD, :])
        sbuf[:, b0:B_PAD, :] = jnp.where(first, 0.0, sbuf[:, tm + b0:tm + B_PAD, :])
        sa[0, A_PAD:A_PAD + tm, :] = ua
        sbuf[0, B_PAD:B_PAD + tm, :] = ub
        sb_scr[0] = sb
        r['na'][...] = sa[:, tm + a0:tm + A_PAD, :]
        r['nb'][...] = sbuf[:, tm + b0:tm + B_PAD, :]
        n_conv = tm // conv_rows
        gate_after = {}
        for g in range(len(gate_cols)):
            gate_after.setdefault(g * n_conv // len(gate_cols), []).append(gate_cols[g])

        def after_chunk(k):
            for c in gate_after.get(k, []):
                gate(c)

        _conv_chunks(sa, sbuf, lambda r0: sb_scr[:, pl.ds(r0, conv_rows), :], r['conv_a_w'], r['conv_b_w'],
                     r['conv_b_bias'][...], r['cf_norm_g'][...], r['cf_norm_b'][...], r['za'], r['zb'],
                     t=tm, rows=conv_rows, unrolled=True, after_chunk=after_chunk)
    else:
        r['q'][...] = _nn(h, r['wq'][...]) * (HEAD_DIM ** -0.5)
        r['k'][last] = _nn(h, r['wk'][...])
        r['v'][last] = _nn(h, r['wv'][...])
        r['lf'][...] = lf
        r['sb'][...] = sb.astype(BF16)
        r['ua'][...] = ua
        r['ub'][...] = ub
        for c in gate_cols:
            gate(c)


def _inproj(x2d, w, prev, *, paged, tm, tiles_per_seq=1, conv_rows=64, casts=None):
    casts = casts or {}
    n, d = x2d.shape
    assert n % tm == 0 and tm % PAGE == 0
    steps = n // tm
    layers = 1 if prev is None else prev[0].shape[0] + 1
    row = lambda width, dt: (jax.ShapeDtypeStruct((n, width), dt), pl.BlockSpec((tm, width), lambda i: (i, 0)))
    row_l = lambda nl, width, dt: (jax.ShapeDtypeStruct((nl, n, width), dt),
                                   pl.BlockSpec((nl, tm, width), lambda i: (0, i, 0)))
    const = lambda a: (a, _const_spec(a.shape))
    ins = {'x': (x2d, pl.BlockSpec((tm, d), lambda i: (i, 0))), 'g': const(w['norm1_g']), 'wf': const(w['wf']),
           'bf': const(w['bf']), 'wmix': const(w['wmix']), 'wgate': const(w['wgate'])}
    scratch = {}
    if paged:
        npg = tm // PAGE
        pg = lambda rows, dt: (jax.ShapeDtypeStruct((n // PAGE, rows, PAGE), dt),
                               pl.BlockSpec((npg, rows, PAGE), lambda i: (i, 0, 0)))
        pg_l = lambda nl, rows, dt: (jax.ShapeDtypeStruct((nl, n // PAGE, rows, PAGE), dt),
                                     pl.BlockSpec((nl, npg, rows, PAGE), lambda i: (0, i, 0, 0)))
        tile = lambda width: (jax.ShapeDtypeStruct((steps, tm, width), BF16),
                              pl.BlockSpec((1, tm, width), lambda i: (i, 0, 0)))
        state = lambda rows, width: (jax.ShapeDtypeStruct((steps // tiles_per_seq, rows, width), F32),
                                     pl.BlockSpec((1, rows, width), lambda i: (i // tiles_per_seq, 0, 0)))
        for name in ('wq', 'wk', 'wv'):
            ins[name] = const(w[name + '_t'])
        for name in ('conv_a_w', 'conv_b_w', 'conv_b_bias', 'cf_norm_g', 'cf_norm_b'):
            ins[name] = const(w[name])
        outs = {'q': pg(ATTN_WIDTH, BF16), 'k': pg_l(layers, ATTN_WIDTH, F32), 'v': pg_l(layers, ATTN_WIDTH, F32),
                'kb': row(ATTN_WIDTH, BF16), 'vb': pg(ATTN_WIDTH, BF16), 'lf': pg_l(layers, N_HEADS, F32),
                'za': tile(SC_WIDTH), 'zb': tile(CF_WIDTH), 'na': state(SC_KERNEL - 1, SC_WIDTH),
                'nb': state(CF_KERNEL - 1, CF_WIDTH)}
        prev_specs = {'k_prev': pg_l(layers - 1, ATTN_WIDTH, F32), 'v_prev': pg_l(layers - 1, ATTN_WIDTH, F32),
                      'lf_prev': pg_l(layers - 1, N_HEADS, F32)}
        scratch = {'sa': pltpu.VMEM((1, tm + A_WIN, SC_WIDTH), F32), 'sbuf': pltpu.VMEM((1, tm + B_WIN, CF_WIDTH), F32),
                   'sb_scr': pltpu.VMEM((1, tm, SC_WIDTH), F32)}
    else:
        for name in ('wq', 'wk', 'wv'):
            ins[name] = const(w[name])
        outs = {'q': row(ATTN_WIDTH, F32), 'k': row_l(layers, ATTN_WIDTH, F32), 'v': row_l(layers, ATTN_WIDTH, F32),
                'lf': row(LANES, F32), 'sb': row(SC_WIDTH, BF16), 'ua': row(SC_WIDTH, F32), 'ub': row(CF_WIDTH, F32)}
        prev_specs = {'k_prev': row_l(layers - 1, ATTN_WIDTH, F32), 'v_prev': row_l(layers - 1, ATTN_WIDTH, F32)}
    outs['gate'] = row(N_BRANCHES * d, BF16)
    for name, a in casts.items():
        every = 1 if (a.shape[1] // steps) % 16 == 0 and a.shape[1] % steps == 0 else 2
        assert a.shape[1] % (steps // every) == 0 and (a.shape[1] * every // steps) % 16 == 0
        spec = pl.BlockSpec((a.shape[0], a.shape[1] * every // steps, a.shape[2]),
                            lambda i, every=every: (0, i // every, 0))
        ins[name + '_f32'] = (a, spec)
        outs[name + '_bf16'] = (jax.ShapeDtypeStruct(a.shape, BF16), spec)
    if prev is not None:
        for name, arr in zip(prev_specs, prev):
            ins[name] = (arr, prev_specs[name][1])
    names = list(ins) + list(outs) + list(scratch)

    def body(*refs):
        _inproj_body(dict(zip(names, refs)), paged=paged, tiles_per_seq=tiles_per_seq, conv_rows=conv_rows,
                     casts=list(casts))

    res = pl.pallas_call(
        body,
        grid=(steps,),
        in_specs=[v[1] for v in ins.values()],
        out_specs=[v[1] for v in outs.values()],
        out_shape=[v[0] for v in outs.values()],
        scratch_shapes=list(scratch.values()),
        compiler_params=pltpu.CompilerParams(dimension_semantics=("arbitrary",), vmem_limit_bytes=VMEM_LIMIT),
        name="inproj_paged" if paged else "inproj_rows",
    )(*[v[0] for v in ins.values()])
    return dict(zip(outs, res))


SUBLANES = 8
A_PAD = 8
B_PAD = 32
A_WIN = 8
B_WIN = 40


def _shift_rows(x, first, n):
    if first % SUBLANES == 0:
        return x[:, first:first + n, :]
    total = x.shape[1]
    return pltpu.roll(x, total - first, 1)[:, :n, :]


def _mixers_body(ua_ref, ub_ref, sb_ref, ha_ref, hb_ref, wa_ref, wb_ref, bias_ref, lng_ref, lnb_ref,
                 za_ref, zb_ref, na_ref, nb_ref, sa, sbuf, *, rows):
    bb, t, _ = ua_ref.shape
    a0 = A_PAD - (SC_KERNEL - 1)
    b0 = B_PAD - (CF_KERNEL - 1)
    sa[:, a0:A_PAD, :] = ha_ref[...]
    sa[:, A_PAD:A_PAD + t, :] = ua_ref[...]
    sbuf[:, b0:B_PAD, :] = hb_ref[...]
    sbuf[:, B_PAD:B_PAD + t, :] = ub_ref[...]
    sbuf[:, B_PAD + t:, :] = jnp.zeros((bb, B_WIN - B_PAD, CF_WIDTH), F32)
    sa[:, 0:a0, :] = jnp.zeros((bb, a0, SC_WIDTH), F32)
    sbuf[:, 0:b0, :] = jnp.zeros((bb, b0, CF_WIDTH), F32)
    na_ref[...] = sa[:, t + a0:t + A_PAD, :]
    nb_ref[...] = sbuf[:, t + b0:t + B_PAD, :]
    _conv_chunks(sa, sbuf, lambda r0: sb_ref[:, pl.ds(r0, rows), :].astype(F32), wa_ref, wb_ref, bias_ref[...],
                 lng_ref[...], lnb_ref[...], za_ref, zb_ref, t=t, rows=rows, unrolled=False)


def _conv_chunks(sa, sbuf, sb_at, wa_ref, wb_ref, bias, lng, lnb, za_ref, zb_ref, *, t, rows, unrolled,
                 after_chunk=None):
    a0 = A_PAD - (SC_KERNEL - 1)
    b0 = B_PAD - (CF_KERNEL - 1)

    def convs(r0):
        halves = []
        for c in range(0, CF_WIDTH, LANES):
            win = sbuf[:, pl.ds(r0, rows + B_WIN), c:c + LANES]
            acc = None
            for b in range(SUBLANES):
                n_a = (CF_KERNEL - 1 - b) // SUBLANES + 1
                shifted = _shift_rows(win, b0 + b, rows + SUBLANES * (n_a - 1))
                for a in range(n_a):
                    i = SUBLANES * a + b
                    term = wb_ref[i:i + 1, c:c + LANES] * shifted[:, SUBLANES * a:SUBLANES * a + rows, :]
                    acc = term if acc is None else acc + term
            halves.append(acc)
        win_a = sa[:, pl.ds(r0, rows + A_WIN), :]
        acc_a = wa_ref[0:1, :] * _shift_rows(win_a, a0, rows)
        for i in range(1, SC_KERNEL):
            acc_a = acc_a + wa_ref[i:i + 1, :] * _shift_rows(win_a, a0 + i, rows)
        za_ref[:, pl.ds(r0, rows), :] = (sb_at(r0) * acc_a).astype(BF16)
        return jnp.concatenate(halves, axis=-1)

    def norm_act(z, r0):
        z = z + bias
        mu = jnp.mean(z, axis=-1, keepdims=True)
        zc = z - mu
        var = jnp.mean(zc * zc, axis=-1, keepdims=True)
        y = zc * lax.rsqrt(var + EPS) * lng + lnb
        zb_ref[:, pl.ds(r0, rows), :] = (y * _sigmoid(y)).astype(BF16)

    if t == rows:
        norm_act(convs(0), 0)
    elif unrolled:
        z_prev = convs(0)
        for r0 in range(rows, t, rows):
            if after_chunk is not None:
                after_chunk(r0 // rows - 1)
            z = convs(r0)
            norm_act(z_prev, r0 - rows)
            z_prev = z
        norm_act(z_prev, t - rows)
        if after_chunk is not None:
            after_chunk(t // rows - 1)
    else:
        def step(c, z_prev):
            r0 = pl.multiple_of(c * rows, rows)
            z = convs(r0)
            norm_act(z_prev, pl.multiple_of((c - 1) * rows, rows))
            return z
        z_last = lax.fori_loop(1, t // rows, step, convs(0))
        norm_act(z_last, t - rows)


def _mixers(ua, ub, sb, hist_a, hist_b, w, *, bb, rows):
    b, t, _ = ua.shape
    assert b % bb == 0 and t % rows == 0
    seq = lambda width: pl.BlockSpec((bb, t, width), lambda i: (i, 0, 0))
    hist = lambda r, width: pl.BlockSpec((bb, r, width), lambda i: (i, 0, 0))
    consts = [w['conv_a_w'], w['conv_b_w'], w['conv_b_bias'], w['cf_norm_g'], w['cf_norm_b']]
    return pl.pallas_call(
        functools.partial(_mixers_body, rows=rows),
        grid=(b // bb,),
        in_specs=[seq(SC_WIDTH), seq(CF_WIDTH), seq(SC_WIDTH), hist(SC_KERNEL - 1, SC_WIDTH),
                  hist(CF_KERNEL - 1, CF_WIDTH)] + [_const_spec(a.shape) for a in consts],
        out_specs=[seq(SC_WIDTH), seq(CF_WIDTH), hist(SC_KERNEL - 1, SC_WIDTH), hist(CF_KERNEL - 1, CF_WIDTH)],
        out_shape=[jax.ShapeDtypeStruct((b, t, SC_WIDTH), BF16), jax.ShapeDtypeStruct((b, t, CF_WIDTH), BF16),
                   jax.ShapeDtypeStruct((b, SC_KERNEL - 1, SC_WIDTH), F32),
                   jax.ShapeDtypeStruct((b, CF_KERNEL - 1, CF_WIDTH), F32)],
        scratch_shapes=[pltpu.VMEM((bb, t + A_WIN, SC_WIDTH), F32), pltpu.VMEM((bb, t + B_WIN, CF_WIDTH), F32)],
        compiler_params=pltpu.CompilerParams(dimension_semantics=("arbitrary",), vmem_limit_bytes=VMEM_LIMIT),
        name=f"mixers_t{t}",
    )(ua, ub, sb, hist_a, hist_b, *consts)


def _tri(n, fn):
    r = lax.broadcasted_iota(jnp.int32, (n, n), 0)
    c = lax.broadcasted_iota(jnp.int32, (n, n), 1)
    return jnp.where(fn(r, c), 1.0, 0.0).astype(BF16)


def _prompt_attn_body(qt_ref, k_ref, vt_ref, lft_ref, o_ref, ct_scr, ccol_scr, crep_scr, m_scr, l_scr, acc_scr,
                      *, tq, heads_per_step):
    npages = qt_ref.shape[0]
    t = npages * PAGE
    ppb = tq // PAGE
    x = lft_ref[...].reshape(npages * N_HEADS, PAGE)
    cw = _exact_nn(x, _tri(PAGE, lambda r, c: r <= c))
    run = jnp.zeros((N_HEADS, 1), F32)
    zpad = jnp.zeros((LANES - N_HEADS, PAGE), F32)
    for p in range(npages):
        cp = cw[p * N_HEADS:(p + 1) * N_HEADS, :] + run
        run = cp[:, PAGE - 1:PAGE]
        cp2 = cp * LOG2E
        ct_scr[p] = cp2
        ccol_scr[p * PAGE:(p + 1) * PAGE, :] = jnp.concatenate([cp2, zpad], axis=0).T
    key_i = lax.broadcasted_iota(jnp.int32, (tq, tq), 0)
    qry_i = lax.broadcasted_iota(jnp.int32, (tq, tq), 1)
    for h in range(N_HEADS):
        for r in range(0, t, tq):
            crep_scr[h, r:r + tq, :] = jnp.broadcast_to(ccol_scr[r:r + tq, h:h + 1], (tq, LANES))

    for h0 in range(0, N_HEADS, heads_per_step):
        heads = range(h0, h0 + heads_per_step)

        def q_block(qi, carry):
            q0 = pl.multiple_of(qi * tq, tq)
            for h in heads:
                m_scr[h] = jnp.full((1, tq), -jnp.inf, F32)
                l_scr[h] = jnp.zeros((1, tq), F32)
                acc_scr[h] = jnp.zeros((HEAD_DIM, tq), F32)

            def k_step(blocks):
                s, pe, alpha = {}, {}, {}

                def scores(kj, part, h):
                    (kp0, kp1), (qp0, qp1), masked = part
                    k_len, q_len = (kp1 - kp0) * PAGE, (qp1 - qp0) * PAGE
                    k0 = pl.multiple_of(kj * tq, tq) + kp0 * PAGE
                    pair = slice((h // 2) * 2 * HEAD_DIM, (h // 2 + 1) * 2 * HEAD_DIM)
                    qh = jnp.concatenate([qt_ref[qi * ppb + j, h * HEAD_DIM:(h + 1) * HEAD_DIM, :]
                                          for j in range(qp0, qp1)], axis=1)
                    zq = jnp.zeros((HEAD_DIM, q_len), BF16)
                    qtz = jnp.concatenate([qh, zq] if h % 2 == 0 else [zq, qh], axis=0)
                    cs = crep_scr[h, pl.ds(k0, k_len), :]
                    sh = _nn(k_ref[pl.ds(k0, k_len), pair], qtz) - jnp.concatenate([cs] * (qp1 - qp0), axis=1)
                    if masked:
                        causal = (lax.broadcasted_iota(jnp.int32, (k_len, q_len), 0)
                                  <= lax.broadcasted_iota(jnp.int32, (k_len, q_len), 1))
                        sh = jnp.where(causal, sh, -jnp.inf)
                    return sh

                def softmax(sh, part, h):
                    qp0, qp1 = part[1]
                    lanes = slice(qp0 * PAGE, qp1 * PAGE)
                    ct = jnp.concatenate([ct_scr[qi * ppb + j, h:h + 1, :] for j in range(qp0, qp1)], axis=1)
                    m_row, l_row = m_scr[h], l_scr[h]
                    m = m_row[:, lanes]
                    m_new = jnp.maximum(m, jnp.max(sh, axis=0, keepdims=True) + ct)
                    a = jnp.exp2(m - m_new)
                    p = jnp.exp2(sh + (ct - m_new))
                    l_new = a * l_row[:, lanes] + jnp.sum(p, axis=0, keepdims=True)
                    def put(row, new):
                        parts = ([row[:, :lanes.start]] if lanes.start else []) + [new]
                        parts += [row[:, lanes.stop:]] if lanes.stop < tq else []
                        return jnp.concatenate(parts, axis=1)

                    m_scr[h] = put(m_row, m_new)
                    l_scr[h] = put(l_row, l_new)
                    return a, p.astype(BF16)

                def weighted_values(kj, part, h, a, p):
                    (kp0, kp1), (qp0, qp1), _ = part
                    lanes = slice(qp0 * PAGE, qp1 * PAGE)
                    vt = jnp.concatenate([vt_ref[kj * ppb + j, h * HEAD_DIM:(h + 1) * HEAD_DIM, :]
                                          for j in range(kp0, kp1)], axis=1)
                    acc_scr[h, :, lanes] = acc_scr[h, :, lanes] * a + _nn(vt, p)

                todo = [(kj, part, h) for kj, part in blocks for h in heads]
                for i in range(len(todo) + 2):
                    if i < len(todo):
                        s[i] = scores(*todo[i])
                    if 0 <= i - 2 < len(todo):
                        weighted_values(*todo[i - 2], alpha.pop(i - 2), pe.pop(i - 2))
                    if 0 <= i - 1 < len(todo):
                        alpha[i - 1], pe[i - 1] = softmax(s.pop(i - 1), *todo[i - 1][1:])

            full = ((0, ppb), (0, ppb), False)
            if ppb % 2 == 0:
                diagonal = [((0, ppb // 2), (0, ppb), True), ((ppb // 2, ppb), (ppb // 2, ppb), True)]
            else:
                diagonal = [((0, ppb), (0, ppb), True)]

            def pair_step(kp, c):
                k_step([(2 * kp, full), (2 * kp + 1, full)])
                return c

            def tail_after_odd(_, c):
                k_step([(qi - 1, full)] + [(qi, part) for part in diagonal])
                return c

            def tail_after_even(_, c):
                k_step([(qi, part) for part in diagonal])
                return c

            lax.fori_loop(0, qi // 2, pair_step, 0)
            lax.fori_loop(0, qi % 2, tail_after_odd, 0)
            lax.fori_loop(0, 1 - qi % 2, tail_after_even, 0)
            for hp in range(h0 // 2, (h0 + heads_per_step) // 2):
                ot = jnp.concatenate([acc_scr[2 * hp + e] / l_scr[2 * hp + e] for e in range(2)], axis=0)
                o_ref[pl.ds(q0, tq), hp * 2 * HEAD_DIM:(hp + 1) * 2 * HEAD_DIM] = ot.T.astype(o_ref.dtype)
            return carry

        lax.fori_loop(0, t // tq, q_block, 0)


ATTN_HEADS_PER_STEP = 8


def _prompt_attn(qt, kb, vtb, lft_layers, *, batch, tq):
    n, _ = kb.shape
    t = n // batch
    npages = t // PAGE
    layer = lft_layers.shape[0] - 1
    assert t % tq == 0 and tq % PAGE == 0
    paged_spec = pl.BlockSpec((npages, ATTN_WIDTH, PAGE), lambda b: (b, 0, 0))
    return pl.pallas_call(
        functools.partial(_prompt_attn_body, tq=tq, heads_per_step=ATTN_HEADS_PER_STEP),
        grid=(batch,),
        in_specs=[paged_spec, pl.BlockSpec((t, ATTN_WIDTH), lambda b: (b, 0)), paged_spec,
                  pl.BlockSpec((None, npages, N_HEADS, PAGE), lambda b: (layer, b, 0, 0))],
        out_specs=pl.BlockSpec((t, ATTN_WIDTH), lambda b: (b, 0)),
        out_shape=jax.ShapeDtypeStruct((n, ATTN_WIDTH), BF16),
        scratch_shapes=[pltpu.VMEM((npages, N_HEADS, PAGE), F32), pltpu.VMEM((t, LANES), F32),
                        pltpu.VMEM((N_HEADS, t, LANES), F32), pltpu.VMEM((N_HEADS, 1, tq), F32),
                        pltpu.VMEM((N_HEADS, 1, tq), F32), pltpu.VMEM((N_HEADS, HEAD_DIM, tq), F32)],
        compiler_params=pltpu.CompilerParams(dimension_semantics=("arbitrary",), vmem_limit_bytes=VMEM_LIMIT),
        name="prompt_attn",
    )(qt, kb, vtb, lft_layers)


def _bias_rows(r_scr, blk, tn):
    return jnp.concatenate(
        [jnp.broadcast_to(r_scr[blk * N_HEADS + h:blk * N_HEADS + h + 1, :], (tn, PAGE)) for h in range(N_HEADS)],
        axis=0)


def _sample_setup(q, lft_ref, lfn, r_scr, npages):
    tn = q.shape[0]
    hq = N_HEADS * tn
    row = lax.broadcasted_iota(jnp.int32, (hq, ATTN_WIDTH), 0)
    lane = lax.broadcasted_iota(jnp.int32, (hq, ATTN_WIDTH), 1)
    own_head = (lane // HEAD_DIM) == (row // tn)
    qbd = jnp.where(own_head, jnp.concatenate([q] * N_HEADS, axis=0), 0.0).astype(BF16)
    nblk = npages + 1
    x = jnp.concatenate([lft_ref[p] for p in range(npages)] + [lfn], axis=0)
    rw = _exact_nn(x, _tri(PAGE, lambda r, c: r > c))
    tot = rw[:, 0:1] + x[:, 0:1]
    run = jnp.zeros((N_HEADS, 1), F32)
    for p in reversed(range(nblk)):
        hs = slice(p * N_HEADS, (p + 1) * N_HEADS)
        r_scr[hs, :] = rw[hs, :] + run
        run = run + tot[hs, :]
    row_p = lax.broadcasted_iota(jnp.int32, (hq, PAGE), 0) % tn
    lane_p = lax.broadcasted_iota(jnp.int32, (hq, PAGE), 1)
    b_new = _bias_rows(r_scr, npages, tn)
    s_q = jnp.sum(jnp.where(lane_p == row_p, b_new, 0.0), axis=-1, keepdims=True)
    return qbd, own_head, b_new, s_q, lane_p <= row_p


def _sample_scores(qbd, s_q, k_pages, page0, r_scr, s_scr, m_prev, new):
    hq = qbd.shape[0]
    tn = hq // N_HEADS
    hp = k_pages.shape[0]
    m = m_prev
    for pp in range(0, hp, 2):
        kt2 = jnp.concatenate([k_pages[pp], k_pages[pp + 1]], axis=1).astype(BF16)
        bias = jnp.concatenate([_bias_rows(r_scr, page0 + pp, tn), _bias_rows(r_scr, page0 + pp + 1, tn)], axis=1)
        s = _nn(qbd, kt2) + bias - s_q
        s_scr[:, pp * PAGE:(pp + 2) * PAGE] = s
        m = jnp.maximum(m, jnp.max(s, axis=-1, keepdims=True))
    if new is not None:
        kn, vn, b_new, causal = new
        s = jnp.where(causal, _nt(qbd, kn) + b_new - s_q, -jnp.inf)
        s_scr[:, hp * PAGE:(hp + 1) * PAGE] = s
        m = jnp.maximum(m, jnp.max(s, axis=-1, keepdims=True))
    return m


def _sample_accumulate(v_pages, s_scr, state, m, new):
    m_prev, l_prev, acc_prev = state
    hp = v_pages.shape[0]
    alpha = jnp.exp(m_prev - m)
    l = alpha * l_prev
    acc = alpha * acc_prev
    for pp in range(0, hp, 2):
        pe = jnp.exp(s_scr[:, pp * PAGE:(pp + 2) * PAGE] - m)
        l = l + jnp.sum(pe, axis=-1, keepdims=True)
        vt2 = jnp.concatenate([v_pages[pp], v_pages[pp + 1]], axis=1).astype(BF16)
        acc = acc + _nt(pe.astype(BF16), vt2)
    if new is not None:
        vn = new[1]
        pe = jnp.exp(s_scr[:, hp * PAGE:(hp + 1) * PAGE] - m)
        l = l + jnp.sum(pe, axis=-1, keepdims=True)
        acc = acc + _nn(pe.astype(BF16), vn)
    return m, l, acc


FFN_CHUNK = 256
SEQ_HALVES = 2
RING_SLOTS = 3
PAGE_DMA_PRIORITY = 1


def _outproj_math(x_ref, za_ref, zb_ref, oc_ref, gate_ref, wa_ref, wb_ref, wc_ref, wo_ref, g2_ref,
                  wg_ref, wu_ref, wd_ref, gf_ref, y_ref, *, final, before_chunk=None, after_chunk=None):
    d = x_ref.shape[1]
    ya = _nn(za_ref[...].astype(BF16), wa_ref[...])
    yb = _nn(zb_ref[...].astype(BF16), wb_ref[...])
    yc = _nn(oc_ref[...].astype(BF16), wc_ref[...])
    merged = (gate_ref[:, 0:d].astype(F32) * ya + gate_ref[:, d:2 * d].astype(F32) * yb
              + gate_ref[:, 2 * d:3 * d].astype(F32) * yc)
    x1 = x_ref[...] + _nn(merged.astype(BF16), wo_ref[...])
    h2 = _rms(x1, g2_ref[...]).astype(BF16)
    acc = x1
    for k, c in enumerate(range(0, wg_ref.shape[1], FFN_CHUNK)):
        if before_chunk is not None:
            before_chunk(k)
        gch = _nn(h2, wg_ref[:, c:c + FFN_CHUNK])
        uch = _nn(h2, wu_ref[:, c:c + FFN_CHUNK])
        acc = acc + _nn((gch * _sigmoid(gch) * uch).astype(BF16), wd_ref[c:c + FFN_CHUNK, :])
        if after_chunk is not None:
            after_chunk(k)
    y_ref[...] = _rms(acc, gf_ref[...]) if final else acc


def _outproj_body(*refs, final):
    _outproj_math(*refs, final=final)


def _outproj_attn_body(pt_ref, *refs, final, layer, npages, nseq):
    proj_refs = refs[:14]
    q_ref, kn_ref, vn_ref, lfn_ref, ckt_hbm, cvt_hbm, clft_hbm, y_ref, os_ref = refs[14:23]
    kring, vring, lring, sems, s_scr, r_scr = refs[23:]
    i = pl.program_id(0)
    sps, tn, _ = q_ref.shape
    hp = npages // SEQ_HALVES
    hq = N_HEADS * tn

    def slot_of(seq, c):
        return (seq * SEQ_HALVES + c) % RING_SLOTS

    def copies(seq, c, slot, parity):
        out = []
        if c == 0:
            for p in range(npages):
                pid = pt_ref[seq * npages + p]
                out.append(pltpu.make_async_copy(clft_hbm.at[layer, pid], lring.at[parity, p], sems.at[parity, 2]))
        for p in range(hp):
            pid = pt_ref[seq * npages + c * hp + p]
            out.append(pltpu.make_async_copy(ckt_hbm.at[layer, pid], kring.at[slot, p], sems.at[slot, 0]))
        for p in range(hp):
            pid = pt_ref[seq * npages + c * hp + p]
            out.append(pltpu.make_async_copy(cvt_hbm.at[layer, pid], vring.at[slot, p], sems.at[slot, 1]))
        return out

    def fetch_after(j, c):
        seq = i * sps + j
        ahead = c + RING_SLOTS
        tgt_seq = jnp.minimum(seq + ahead // SEQ_HALVES, nseq - 1)
        return copies(tgt_seq, ahead % SEQ_HALVES, slot_of(seq, c), (j + ahead // SEQ_HALVES) % 2)

    @pl.when(i == 0)
    def _():
        for g in range(RING_SLOTS):
            for cp in copies(g // SEQ_HALVES, g % SEQ_HALVES, g, (g // SEQ_HALVES) % 2):
                cp.start(priority=PAGE_DMA_PRIORITY)

    items = [(j, c) for j in range(sps) for c in range(SEQ_HALVES)]
    n_ffn = -(-proj_refs[10].shape[1] // FFN_CHUNK)
    slots = [t * n_ffn // len(items) for t in range(len(items))]
    seq_state = {}
    scored = {}

    def score_phase(j, c):
        seq = i * sps + j
        parity = j % 2
        slot = slot_of(seq, c)
        for cp in copies(seq, c, slot, parity):
            cp.wait()
        if c == 0:
            qbd, own_head, b_new, s_q, causal = _sample_setup(q_ref[j], lring.at[parity], lfn_ref[j], r_scr, npages)
            state = (jnp.full((hq, 1), -jnp.inf, F32), jnp.zeros((hq, 1), F32), jnp.zeros((hq, ATTN_WIDTH), F32))
            new = None
        else:
            qbd, own_head, b_new, s_q, causal, state = seq_state.pop(j)
            zrows = jnp.zeros((PAGE - tn, ATTN_WIDTH), F32)
            kn = jnp.concatenate([kn_ref[j], zrows], axis=0).astype(BF16)
            vn = jnp.concatenate([vn_ref[j], zrows], axis=0).astype(BF16)
            new = (kn, vn, b_new, causal)
        m = _sample_scores(qbd, s_q, kring.at[slot], c * hp, r_scr, s_scr, state[0], new)
        scored[(j, c)] = (qbd, own_head, b_new, s_q, causal, state, m, new, slot)

    def accumulate_phase(j, c):
        seq = i * sps + j
        qbd, own_head, b_new, s_q, causal, state, m, new, slot = scored.pop((j, c))
        state = _sample_accumulate(vring.at[slot], s_scr, state, m, new)
        for cp in fetch_after(j, c):
            cp.start(priority=PAGE_DMA_PRIORITY)
        if c == 0:
            seq_state[j] = (qbd, own_head, b_new, s_q, causal, state)
        else:
            _, l, acc = state
            o = jnp.where(own_head, acc / l, 0.0)
            out = o[0:tn, :]
            for h in range(1, N_HEADS):
                out = out + o[h * tn:(h + 1) * tn, :]
            os_ref[j] = out

    def before_chunk(k):
        first = [t for t in range(len(items)) if slots[t] == k][:1]
        for t in first:
            score_phase(*items[t])

    def after_chunk(k):
        mine = [t for t in range(len(items)) if slots[t] == k]
        for n_t, t in enumerate(mine):
            if n_t > 0:
                score_phase(*items[t])
            accumulate_phase(*items[t])

    _outproj_math(*proj_refs, y_ref, final=final, before_chunk=before_chunk, after_chunk=after_chunk)

    @pl.when(i == pl.num_programs(0) - 1)
    def _():
        for j, c in items[-RING_SLOTS:]:
            for cp in fetch_after(j, c):
                cp.wait()


def _outproj_weights(w, gf, layer):
    const = lambda a: (a, pl.BlockSpec(a.shape, lambda *_: (0,) * a.ndim, pipeline_mode=pl.Buffered(1)))
    of_layer = lambda a: (a, pl.BlockSpec((None,) + a.shape[1:], lambda *_: (layer,) + (0,) * (a.ndim - 1),
                                          pipeline_mode=pl.Buffered(1)))
    return [const(w['wa']), const(w['wb']), const(w['wc']), const(w['wo']), const(w['norm2_g']),
            of_layer(w['wg_layers']), of_layer(w['wu_layers']), of_layer(w['wd_layers']), const(gf)]


def _outproj_attn(x2d, za, zb, oc, gates, w, gf, layer, page_table_flat, q, kn, vn, lfn, cache_kt, cache_vt,
                  cache_lft, *, final, tm, npages):
    n, d = x2d.shape
    nseq, tn, _ = q.shape
    steps = n // tm
    assert n % tm == 0 and nseq % steps == 0 and npages % (2 * SEQ_HALVES) == 0
    sps = nseq // steps
    assert sps % 2 == 0
    hp = npages // SEQ_HALVES
    assert nseq * SEQ_HALVES >= RING_SLOTS
    row = lambda a: pl.BlockSpec((tm, a.shape[1]), lambda i, pt: (i, 0))
    seqs = lambda width: pl.BlockSpec((sps, tn, width), lambda i, pt: (i, 0, 0))
    seqs_l = lambda width: pl.BlockSpec((None, sps, tn, width), lambda i, pt: (layer, i, 0, 0))
    hbm = pl.BlockSpec(memory_space=pl.ANY)
    weights = _outproj_weights(w, gf, layer)
    grid_spec = pltpu.PrefetchScalarGridSpec(
        num_scalar_prefetch=1,
        grid=(steps,),
        in_specs=[row(x2d), row(za), row(zb), row(oc), row(gates)] + [s for _, s in weights]
        + [seqs(ATTN_WIDTH), seqs_l(ATTN_WIDTH), seqs_l(ATTN_WIDTH),
           pl.BlockSpec((sps, N_HEADS, PAGE), lambda i, pt: (i, 0, 0)), hbm, hbm, hbm],
        out_specs=[pl.BlockSpec((tm, d), lambda i, pt: (i, 0)), seqs(ATTN_WIDTH)],
        scratch_shapes=[pltpu.VMEM((RING_SLOTS, hp, ATTN_WIDTH, PAGE), F32),
                        pltpu.VMEM((RING_SLOTS, hp, ATTN_WIDTH, PAGE), F32),
                        pltpu.VMEM((2, npages, N_HEADS, PAGE), F32),
                        pltpu.SemaphoreType.DMA((RING_SLOTS, 3)),
                        pltpu.VMEM((N_HEADS * tn, (hp + 1) * PAGE), F32),
                        pltpu.VMEM(((npages + 1) * N_HEADS, PAGE), F32)],
    )
    return pl.pallas_call(
        functools.partial(_outproj_attn_body, final=final, layer=layer, npages=npages, nseq=nseq),
        grid_spec=grid_spec,
        out_shape=[jax.ShapeDtypeStruct((n, d), F32), jax.ShapeDtypeStruct((nseq, tn, ATTN_WIDTH), F32)],
        compiler_params=pltpu.CompilerParams(dimension_semantics=("arbitrary",), vmem_limit_bytes=VMEM_LIMIT),
        name="outproj_attn_final" if final else "outproj_attn",
    )(page_table_flat, x2d, za, zb, oc, gates, *[a for a, _ in weights], q, kn, vn, lfn, cache_kt, cache_vt, cache_lft)


def _outproj(x2d, za, zb, oc, gates, w, gf, layer, *, final, tm):
    n, d = x2d.shape
    assert n % tm == 0
    row = lambda a: pl.BlockSpec((tm, a.shape[1]), lambda i: (i, 0))
    weights = _outproj_weights(w, gf, layer)
    return pl.pallas_call(
        functools.partial(_outproj_body, final=final),
        grid=(n // tm,),
        in_specs=[row(x2d), row(za), row(zb), row(oc), row(gates)] + [s for _, s in weights],
        out_specs=pl.BlockSpec((tm, d), lambda i: (i, 0)),
        out_shape=jax.ShapeDtypeStruct((n, d), F32),
        compiler_params=pltpu.CompilerParams(dimension_semantics=("arbitrary",), vmem_limit_bytes=VMEM_LIMIT),
        name="outproj_final" if final else "outproj",
    )(x2d, za, zb, oc, gates, *[a for a, _ in weights])


def _layer_weights(l, norm1_g, w_in, b_f, conv_a_w, conv_b_w, conv_b_bias, cf_norm_g, cf_norm_b,
                   w_a_out, w_b_out, w_c_out, w_o, norm2_g, w_ffn_gate, w_ffn_up, w_ffn_down):
    d = w_in.shape[1]
    wi = w_in[l]
    off_f = 3 * ATTN_WIDTH
    off_mix = off_f + N_HEADS
    off_gate = off_mix + 3 * SC_WIDTH + 2 * CF_WIDTH
    wq, wk, wv = (wi[:, i * ATTN_WIDTH:(i + 1) * ATTN_WIDTH].astype(BF16) for i in range(3))
    pad = LANES - N_HEADS
    return {
        'norm1_g': norm1_g[l].reshape(1, d),
        'wq': wq, 'wk': wk, 'wv': wv, 'wq_t': wq.T, 'wk_t': wk.T, 'wv_t': wv.T,
        'wf': jnp.pad(wi[:, off_f:off_mix], ((0, 0), (0, pad))).astype(BF16),
        'bf': jnp.pad(b_f[l], (0, pad)).reshape(1, LANES),
        'wmix': wi[:, off_mix:off_gate].astype(BF16),
        'wgate': wi[:, off_gate:].astype(BF16),
        'conv_a_w': conv_a_w[l], 'conv_b_w': conv_b_w[l], 'conv_b_bias': conv_b_bias[l].reshape(1, -1),
        'cf_norm_g': cf_norm_g[l].reshape(1, -1), 'cf_norm_b': cf_norm_b[l].reshape(1, -1),
        'wa': w_a_out[l].astype(BF16), 'wb': w_b_out[l].astype(BF16), 'wc': w_c_out[l].astype(BF16),
        'wo': w_o[l].astype(BF16), 'norm2_g': norm2_g[l].reshape(1, d),
    }


def kernel(x_prompt, x_sample, cache_k, cache_v, cache_logf, state_conv_a, state_conv_b, page_table, norm1_g, w_in, b_f, conv_a_w, conv_b_w, conv_b_bias, cf_norm_g, cf_norm_b, w_a_out, w_b_out, w_c_out, w_o, norm2_g, w_ffn_gate, w_ffn_up, w_ffn_down, final_norm_g):
    depth = w_in.shape[0]
    pb, ps, d = x_prompt.shape
    sb_, st, _ = x_sample.shape
    n_pool = cache_k.shape[1]
    npages = page_table.shape[1]
    assert cache_k.shape[2:] == (PAGE, N_HEADS, HEAD_DIM) and ps % PAGE == 0
    cache_kt = cache_k.transpose(0, 1, 3, 4, 2).reshape(depth, n_pool, ATTN_WIDTH, PAGE)
    cache_vt = cache_v.transpose(0, 1, 3, 4, 2).reshape(depth, n_pool, ATTN_WIDTH, PAGE)
    cache_lft = cache_logf.transpose(0, 1, 3, 2)
    pt_flat = page_table.reshape(-1)
    gf = final_norm_g.reshape(1, d)
    tm_p = 512 if ps % 512 == 0 else PAGE
    tm_s = 512 if (sb_ * st) % 512 == 0 else sb_ * st
    tq = 256 if ps % 256 == 0 else PAGE
    bb = 16 if sb_ % 16 == 0 else sb_
    conv_rows = 64 if ps % 64 == 0 else ps

    xp = x_prompt.reshape(pb * ps, d)
    xs = x_sample.reshape(sb_ * st, d)
    outs = {k: [] for k in ('ap', 'bp', 'fs', 'as', 'bs')}
    prev_p = prev_s = None
    npp = ps // PAGE
    for l in range(depth):
        w = _layer_weights(l, norm1_g, w_in, b_f, conv_a_w, conv_b_w, conv_b_bias, cf_norm_g, cf_norm_b,
                           w_a_out, w_b_out, w_c_out, w_o, norm2_g, w_ffn_gate, w_ffn_up, w_ffn_down)
        final = l == depth - 1
        o_s = _inproj(xs, w, prev_s, paged=False, tm=tm_s)
        prev_s = (o_s['k'], o_s['v'])
        za_s, zb_s, na_s, nb_s = _mixers(o_s['ua'].reshape(sb_, st, -1), o_s['ub'].reshape(sb_, st, -1),
                                         o_s['sb'].reshape(sb_, st, -1), state_conv_a[l], state_conv_b[l], w,
                                         bb=bb, rows=st)
        lf = o_s['lf'][:, :N_HEADS].reshape(sb_, st, N_HEADS)
        lfn = jnp.pad(lf.transpose(0, 2, 1), ((0, 0), (0, 0), (0, PAGE - st)))
        casts = {'wg': w_ffn_gate, 'wu': w_ffn_up, 'wd': w_ffn_down} if l == 0 else None
        o_p = _inproj(xp, w, prev_p, paged=True, tm=tm_p, tiles_per_seq=ps // tm_p, conv_rows=conv_rows, casts=casts)
        if l == 0:
            ffn = {name + '_layers': o_p[name + '_bf16'] for name in casts}
        w.update(ffn)
        prev_p = (o_p['k'], o_p['v'], o_p['lf'])
        oc = _prompt_attn(o_p['q'], o_p['kb'], o_p['vb'], o_p['lf'], batch=pb, tq=tq)
        xp, oc_s = _outproj_attn(xp, o_p['za'].reshape(pb * ps, -1), o_p['zb'].reshape(pb * ps, -1), oc, o_p['gate'],
                                 w, gf, l, pt_flat, o_s['q'].reshape(sb_, st, -1),
                                 o_s['k'].reshape(l + 1, sb_, st, -1), o_s['v'].reshape(l + 1, sb_, st, -1),
                                 lfn, cache_kt, cache_vt, cache_lft, final=final, tm=tm_p, npages=npages)
        xs = _outproj(xs, za_s.reshape(sb_ * st, -1), zb_s.reshape(sb_ * st, -1), oc_s.reshape(sb_ * st, -1),
                      o_s['gate'], w, gf, l, final=final, tm=tm_s)
        outs['ap'].append(o_p['na'])
        outs['bp'].append(o_p['nb'])
        outs['fs'].append(lf)
        outs['as'].append(na_s)
        outs['bs'].append(nb_s)
    stack = lambda k: jnp.stack(outs[k])
    kt, vt, lft = prev_p
    to_pages = lambda a: a.reshape(depth, pb, npp, N_HEADS, HEAD_DIM, PAGE).transpose(0, 1, 2, 5, 3, 4)
    ks, vs = (a.reshape(depth, sb_, st, N_HEADS, HEAD_DIM) for a in prev_s)
    return (xp.reshape(pb, ps, d), xs.reshape(sb_, st, d), to_pages(kt), to_pages(vt),
            lft.reshape(depth, pb, npp, N_HEADS, PAGE).transpose(0, 1, 2, 4, 3), stack('ap'), stack('bp'),
            ks, vs, stack('fs'), stack('as'), stack('bs'))
```

```python
import functools

import jax
import jax.numpy as jnp
from jax import lax
from jax.experimental import pallas as pl
from jax.experimental.pallas import tpu as pltpu

F32 = jnp.float32
BF16 = jnp.bfloat16

EPS = 1e-6
LOG2E = 1.4426950408889634
N_HEADS = 8
HEAD_DIM = 64
ATTN_WIDTH = N_HEADS * HEAD_DIM
PAGE = 128
SC_WIDTH = 256
SC_KERNEL = 3
CF_WIDTH = 256
CF_KERNEL = 31
N_BRANCHES = 3
LANES = 128
VMEM_LIMIT = 56 * 1024 * 1024


def _nt(a, b):
    return lax.dot_general(a, b, (((1,), (1,)), ((), ())), preferred_element_type=F32)


def _nn(a, b):
    return jnp.dot(a, b, preferred_element_type=F32)


def _split3(x):
    hi = x.astype(BF16)
    r1 = x - hi.astype(F32)
    mid = r1.astype(BF16)
    lo = (r1 - mid.astype(F32)).astype(BF16)
    return hi, mid, lo


def _exact_nn(x, w01):
    hi, mid, lo = _split3(x)
    return _nn(hi, w01) + _nn(mid, w01) + _nn(lo, w01)


def _sigmoid(x):
    return 1.0 / (1.0 + jnp.exp(-x))


def _log_sigmoid(x):
    return jnp.minimum(x, 0.0) - jnp.log(1.0 + jnp.exp(-jnp.abs(x)))


def _rms(x, g):
    return x * lax.rsqrt(jnp.mean(x * x, axis=-1, keepdims=True) + EPS) * g


def _const_spec(shape):
    nd = len(shape)
    return pl.BlockSpec(shape, lambda *_: (0,) * nd, pipeline_mode=pl.Buffered(1))


def _inproj_body(r, *, paged, tiles_per_seq, conv_rows, casts):
    h = _rms(r['x'][...], r['g'][...]).astype(BF16)
    tm = h.shape[0]
    for name in casts:
        r[name + '_bf16'][...] = r[name + '_f32'][...].astype(BF16)
    last = r['k'].shape[0] - 1
    for j in range(last):
        r['k'][j] = r['k_prev'][j]
        r['v'][j] = r['v_prev'][j]
        if paged:
            r['lf'][j] = r['lf_prev'][j]

    def mix(c0, c1):
        return _nt(h, r['wmix'][c0:c1, :])

    gw = 512
    gate_cols = list(range(0, r['gate'].shape[1], gw))

    def gate(c):
        r['gate'][:, c:c + gw] = _sigmoid(_nt(h, r['wgate'][c:c + gw, :])).astype(BF16)

    f = _nt(h, r['wf'][...]) + r['bf'][...]
    lf = _log_sigmoid(f)
    sb = mix(0, SC_WIDTH)
    ua = mix(SC_WIDTH, 2 * SC_WIDTH) * mix(2 * SC_WIDTH, 3 * SC_WIDTH)
    c0 = 3 * SC_WIDTH
    ub = mix(c0, c0 + CF_WIDTH) * _sigmoid(mix(c0 + CF_WIDTH, c0 + 2 * CF_WIDTH))
    if paged:
        qt = (_nt(r['wq'][...], h) * (HEAD_DIM ** -0.5 * LOG2E)).astype(BF16)
        kt = _nt(r['wk'][...], h)
        vt = _nt(r['wv'][...], h)
        r['kb'][...] = kt.T.astype(BF16)
        lft = lf.T
        for p in range(tm // PAGE):
            sl = slice(p * PAGE, (p + 1) * PAGE)
            r['q'][p] = qt[:, sl]
            r['k'][last, p] = kt[:, sl]
            r['v'][last, p] = vt[:, sl]
            r['vb'][p] = vt[:, sl].astype(BF16)
            r['lf'][last, p] = lft[:N_HEADS, sl]
        sa, sbuf, sb_scr = r['sa'], r['sbuf'], r['sb_scr']
        a0 = A_PAD - (SC_KERNEL - 1)
        b0 = B_PAD - (CF_KERNEL - 1)
        i = pl.program_id(0)

        @pl.when(i == 0)
        def _():
            sa[...] = jnp.zeros(sa.shape, F32)
            sbuf[...] = jnp.zeros(sbuf.shape, F32)

        first = (i % tiles_per_seq) == 0
        sa[:, a0:A_PAD, :] = jnp.where(first, 0.0, sa[:, tm + a0:tm + A_PAD, :])
        sbuf[:, b0:B_PAD, :] = jnp.where(first, 0.0, sbuf[:, tm + b0:tm + B_PAD, :])
        sa[0, A_PAD:A_PAD + tm, :] = ua
        sbuf[0, B_PAD:B_PAD + tm, :] = ub
        sb_scr[0] = sb
        r['na'][...] = sa[:, tm + a0:tm + A_PAD, :]
        r['nb'][...] = sbuf[:, tm + b0:tm + B_PAD, :]
        n_conv = tm // conv_rows
        gate_after = {}
        for g in range(len(gate_cols)):
            gate_after.setdefault(g * n_conv // len(gate_cols), []).append(gate_cols[g])

        def after_chunk(k):
            for c in gate_after.get(k, []):
                gate(c)

        _conv_chunks(sa, sbuf, lambda r0: sb_scr[:, pl.ds(r0, conv_rows), :], r['conv_a_w'], r['conv_b_w'],
                     r['conv_b_bias'][...], r['cf_norm_g'][...], r['cf_norm_b'][...], r['za'], r['zb'],
                     t=tm, rows=conv_rows, unrolled=True, after_chunk=after_chunk)
    else:
        r['q'][...] = _nt(h, r['wq'][...]) * (HEAD_DIM ** -0.5)
        r['k'][last] = _nt(h, r['wk'][...])
        r['v'][last] = _nt(h, r['wv'][...])
        r['lf'][...] = lf
        r['sb'][...] = sb.astype(BF16)
        r['ua'][...] = ua
        r['ub'][...] = ub
        for c in gate_cols:
            gate(c)


def _inproj(x2d, w, prev, *, paged, tm, tiles_per_seq=1, conv_rows=64, casts=None):
    casts = casts or {}
    n, d = x2d.shape
    assert n % tm == 0 and tm % PAGE == 0
    steps = n // tm
    layers = 1 if prev is None else prev[0].shape[0] + 1
    row = lambda width, dt: (jax.ShapeDtypeStruct((n, width), dt), pl.BlockSpec((tm, width), lambda i: (i, 0)))
    row_l = lambda nl, width, dt: (jax.ShapeDtypeStruct((nl, n, width), dt),
                                   pl.BlockSpec((nl, tm, width), lambda i: (0, i, 0)))
    const = lambda a: (a, _const_spec(a.shape))
    ins = {'x': (x2d, pl.BlockSpec((tm, d), lambda i: (i, 0))), 'g': const(w['norm1_g']), 'wf': const(w['wf_t']),
           'bf': const(w['bf']), 'wmix': const(w['wmix_t']), 'wgate': const(w['wgate_t'])}
    scratch = {}
    if paged:
        npg = tm // PAGE
        pg = lambda rows, dt: (jax.ShapeDtypeStruct((n // PAGE, rows, PAGE), dt),
                               pl.BlockSpec((npg, rows, PAGE), lambda i: (i, 0, 0)))
        pg_l = lambda nl, rows, dt: (jax.ShapeDtypeStruct((nl, n // PAGE, rows, PAGE), dt),
                                     pl.BlockSpec((nl, npg, rows, PAGE), lambda i: (0, i, 0, 0)))
        tile = lambda width: (jax.ShapeDtypeStruct((steps, tm, width), BF16),
                              pl.BlockSpec((1, tm, width), lambda i: (i, 0, 0)))
        state = lambda rows, width: (jax.ShapeDtypeStruct((steps // tiles_per_seq, rows, width), F32),
                                     pl.BlockSpec((1, rows, width), lambda i: (i // tiles_per_seq, 0, 0)))
        for name in ('wq', 'wk', 'wv'):
            ins[name] = const(w[name + '_t'])
        for name in ('conv_a_w', 'conv_b_w', 'conv_b_bias', 'cf_norm_g', 'cf_norm_b'):
            ins[name] = const(w[name])
        outs = {'q': pg(ATTN_WIDTH, BF16), 'k': pg_l(layers, ATTN_WIDTH, F32), 'v': pg_l(layers, ATTN_WIDTH, F32),
                'kb': row(ATTN_WIDTH, BF16), 'vb': pg(ATTN_WIDTH, BF16), 'lf': pg_l(layers, N_HEADS, F32),
                'za': tile(SC_WIDTH), 'zb': tile(CF_WIDTH), 'na': state(SC_KERNEL - 1, SC_WIDTH),
                'nb': state(CF_KERNEL - 1, CF_WIDTH)}
        prev_specs = {'k_prev': pg_l(layers - 1, ATTN_WIDTH, F32), 'v_prev': pg_l(layers - 1, ATTN_WIDTH, F32),
                      'lf_prev': pg_l(layers - 1, N_HEADS, F32)}
        scratch = {'sa': pltpu.VMEM((1, tm + A_WIN, SC_WIDTH), F32), 'sbuf': pltpu.VMEM((1, tm + B_WIN, CF_WIDTH), F32),
                   'sb_scr': pltpu.VMEM((1, tm, SC_WIDTH), F32)}
    else:
        for name in ('wq', 'wk', 'wv'):
            ins[name] = const(w[name + '_t'])
        outs = {'q': row(ATTN_WIDTH, F32), 'k': row_l(layers, ATTN_WIDTH, F32), 'v': row_l(layers, ATTN_WIDTH, F32),
                'lf': row(LANES, F32), 'sb': row(SC_WIDTH, BF16), 'ua': row(SC_WIDTH, F32), 'ub': row(CF_WIDTH, F32)}
        prev_specs = {'k_prev': row_l(layers - 1, ATTN_WIDTH, F32), 'v_prev': row_l(layers - 1, ATTN_WIDTH, F32)}
    outs['gate'] = row(N_BRANCHES * d, BF16)
    for name, a in casts.items():
        every = 1 if (a.shape[1] // steps) % 16 == 0 and a.shape[1] % steps == 0 else 2
        assert a.shape[1] % (steps // every) == 0 and (a.shape[1] * every // steps) % 16 == 0
        spec = pl.BlockSpec((a.shape[0], a.shape[1] * every // steps, a.shape[2]),
                            lambda i, every=every: (0, i // every, 0))
        ins[name + '_f32'] = (a, spec)
        outs[name + '_bf16'] = (jax.ShapeDtypeStruct(a.shape, BF16), spec)
    if prev is not None:
        for name, arr in zip(prev_specs, prev):
            ins[name] = (arr, prev_specs[name][1])
    names = list(ins) + list(outs) + list(scratch)

    def body(*refs):
        _inproj_body(dict(zip(names, refs)), paged=paged, tiles_per_seq=tiles_per_seq, conv_rows=conv_rows,
                     casts=list(casts))

    res = pl.pallas_call(
        body,
        grid=(steps,),
        in_specs=[v[1] for v in ins.values()],
        out_specs=[v[1] for v in outs.values()],
        out_shape=[v[0] for v in outs.values()],
        scratch_shapes=list(scratch.values()),
        compiler_params=pltpu.CompilerParams(dimension_semantics=("arbitrary",), vmem_limit_bytes=VMEM_LIMIT),
        name="inproj_paged" if paged else "inproj_rows",
    )(*[v[0] for v in ins.values()])
    return dict(zip(outs, res))


SUBLANES = 8
A_PAD = 8
B_PAD = 32
A_WIN = 8
B_WIN = 40


def _shift_rows(x, first, n):
    if first % SUBLANES == 0:
        return x[:, first:first + n, :]
    total = x.shape[1]
    return pltpu.roll(x, total - first, 1)[:, :n, :]


def _mixers_body(ua_ref, ub_ref, sb_ref, ha_ref, hb_ref, wa_ref, wb_ref, bias_ref, lng_ref, lnb_ref,
                 za_ref, zb_ref, na_ref, nb_ref, sa, sbuf, *, rows):
    bb, t, _ = ua_ref.shape
    a0 = A_PAD - (SC_KERNEL - 1)
    b0 = B_PAD - (CF_KERNEL - 1)
    sa[:, a0:A_PAD, :] = ha_ref[...]
    sa[:, A_PAD:A_PAD + t, :] = ua_ref[...]
    sbuf[:, b0:B_PAD, :] = hb_ref[...]
    sbuf[:, B_PAD:B_PAD + t, :] = ub_ref[...]
    sbuf[:, B_PAD + t:, :] = jnp.zeros((bb, B_WIN - B_PAD, CF_WIDTH), F32)
    sa[:, 0:a0, :] = jnp.zeros((bb, a0, SC_WIDTH), F32)
    sbuf[:, 0:b0, :] = jnp.zeros((bb, b0, CF_WIDTH), F32)
    na_ref[...] = sa[:, t + a0:t + A_PAD, :]
    nb_ref[...] = sbuf[:, t + b0:t + B_PAD, :]
    _conv_chunks(sa, sbuf, lambda r0: sb_ref[:, pl.ds(r0, rows), :].astype(F32), wa_ref, wb_ref, bias_ref[...],
                 lng_ref[...], lnb_ref[...], za_ref, zb_ref, t=t, rows=rows, unrolled=False)


def _conv_chunks(sa, sbuf, sb_at, wa_ref, wb_ref, bias, lng, lnb, za_ref, zb_ref, *, t, rows, unrolled,
                 after_chunk=None):
    a0 = A_PAD - (SC_KERNEL - 1)
    b0 = B_PAD - (CF_KERNEL - 1)

    def convs(r0):
        halves = []
        for c in range(0, CF_WIDTH, LANES):
            win = sbuf[:, pl.ds(r0, rows + B_WIN), c:c + LANES]
            acc = None
            for b in range(SUBLANES):
                n_a = (CF_KERNEL - 1 - b) // SUBLANES + 1
                shifted = _shift_rows(win, b0 + b, rows + SUBLANES * (n_a - 1))
                for a in range(n_a):
                    i = SUBLANES * a + b
                    term = wb_ref[i:i + 1, c:c + LANES] * shifted[:, SUBLANES * a:SUBLANES * a + rows, :]
                    acc = term if acc is None else acc + term
            halves.append(acc)
        win_a = sa[:, pl.ds(r0, rows + A_WIN), :]
        acc_a = wa_ref[0:1, :] * _shift_rows(win_a, a0, rows)
        for i in range(1, SC_KERNEL):
            acc_a = acc_a + wa_ref[i:i + 1, :] * _shift_rows(win_a, a0 + i, rows)
        za_ref[:, pl.ds(r0, rows), :] = (sb_at(r0) * acc_a).astype(BF16)
        return jnp.concatenate(halves, axis=-1)

    def norm_act(z, r0):
        z = z + bias
        mu = jnp.mean(z, axis=-1, keepdims=True)
        zc = z - mu
        var = jnp.mean(zc * zc, axis=-1, keepdims=True)
        y = zc * lax.rsqrt(var + EPS) * lng + lnb
        zb_ref[:, pl.ds(r0, rows), :] = (y * _sigmoid(y)).astype(BF16)

    if t == rows:
        norm_act(convs(0), 0)
    elif unrolled:
        z_prev = convs(0)
        for r0 in range(rows, t, rows):
            if after_chunk is not None:
                after_chunk(r0 // rows - 1)
            z = convs(r0)
            norm_act(z_prev, r0 - rows)
            z_prev = z
        norm_act(z_prev, t - rows)
        if after_chunk is not None:
            after_chunk(t // rows - 1)
    else:
        def step(c, z_prev):
            r0 = pl.multiple_of(c * rows, rows)
            z = convs(r0)
            norm_act(z_prev, pl.multiple_of((c - 1) * rows, rows))
            return z
        z_last = lax.fori_loop(1, t // rows, step, convs(0))
        norm_act(z_last, t - rows)


def _mixers(ua, ub, sb, hist_a, hist_b, w, *, bb, rows):
    b, t, _ = ua.shape
    assert b % bb == 0 and t % rows == 0
    seq = lambda width: pl.BlockSpec((bb, t, width), lambda i: (i, 0, 0))
    hist = lambda r, width: pl.BlockSpec((bb, r, width), lambda i: (i, 0, 0))
    consts = [w['conv_a_w'], w['conv_b_w'], w['conv_b_bias'], w['cf_norm_g'], w['cf_norm_b']]
    return pl.pallas_call(
        functools.partial(_mixers_body, rows=rows),
        grid=(b // bb,),
        in_specs=[seq(SC_WIDTH), seq(CF_WIDTH), seq(SC_WIDTH), hist(SC_KERNEL - 1, SC_WIDTH),
                  hist(CF_KERNEL - 1, CF_WIDTH)] + [_const_spec(a.shape) for a in consts],
        out_specs=[seq(SC_WIDTH), seq(CF_WIDTH), hist(SC_KERNEL - 1, SC_WIDTH), hist(CF_KERNEL - 1, CF_WIDTH)],
        out_shape=[jax.ShapeDtypeStruct((b, t, SC_WIDTH), BF16), jax.ShapeDtypeStruct((b, t, CF_WIDTH), BF16),
                   jax.ShapeDtypeStruct((b, SC_KERNEL - 1, SC_WIDTH), F32),
                   jax.ShapeDtypeStruct((b, CF_KERNEL - 1, CF_WIDTH), F32)],
        scratch_shapes=[pltpu.VMEM((bb, t + A_WIN, SC_WIDTH), F32), pltpu.VMEM((bb, t + B_WIN, CF_WIDTH), F32)],
        compiler_params=pltpu.CompilerParams(dimension_semantics=("arbitrary",), vmem_limit_bytes=VMEM_LIMIT),
        name=f"mixers_t{t}",
    )(ua, ub, sb, hist_a, hist_b, *consts)


def _tri(n, fn):
    r = lax.broadcasted_iota(jnp.int32, (n, n), 0)
    c = lax.broadcasted_iota(jnp.int32, (n, n), 1)
    return jnp.where(fn(r, c), 1.0, 0.0).astype(BF16)


def _prompt_attn_body(qt_ref, k_ref, vt_ref, lft_ref, o_ref, ct_scr, ccol_scr, crep_scr, m_scr, l_scr, acc_scr,
                      *, tq, heads_per_step):
    npages = qt_ref.shape[0]
    t = npages * PAGE
    ppb = tq // PAGE
    x = lft_ref[...].reshape(npages * N_HEADS, PAGE)
    cw = _exact_nn(x, _tri(PAGE, lambda r, c: r <= c))
    run = jnp.zeros((N_HEADS, 1), F32)
    zpad = jnp.zeros((LANES - N_HEADS, PAGE), F32)
    for p in range(npages):
        cp = cw[p * N_HEADS:(p + 1) * N_HEADS, :] + run
        run = cp[:, PAGE - 1:PAGE]
        cp2 = cp * LOG2E
        ct_scr[p] = cp2
        ccol_scr[p * PAGE:(p + 1) * PAGE, :] = jnp.concatenate([cp2, zpad], axis=0).T
    key_i = lax.broadcasted_iota(jnp.int32, (tq, tq), 0)
    qry_i = lax.broadcasted_iota(jnp.int32, (tq, tq), 1)
    for h in range(N_HEADS):
        for r in range(0, t, tq):
            crep_scr[h, r:r + tq, :] = jnp.broadcast_to(ccol_scr[r:r + tq, h:h + 1], (tq, LANES))

    for h0 in range(0, N_HEADS, heads_per_step):
        heads = range(h0, h0 + heads_per_step)

        def q_block(qi, carry):
            q0 = pl.multiple_of(qi * tq, tq)
            for h in heads:
                m_scr[h] = jnp.full((1, tq), -jnp.inf, F32)
                l_scr[h] = jnp.zeros((1, tq), F32)
                acc_scr[h] = jnp.zeros((HEAD_DIM, tq), F32)

            def k_step(blocks):
                s, pe, alpha = {}, {}, {}

                def scores(kj, part, h):
                    (kp0, kp1), (qp0, qp1), masked = part
                    k_len, q_len = (kp1 - kp0) * PAGE, (qp1 - qp0) * PAGE
                    k0 = pl.multiple_of(kj * tq, tq) + kp0 * PAGE
                    pair = slice((h // 2) * 2 * HEAD_DIM, (h // 2 + 1) * 2 * HEAD_DIM)
                    qh = jnp.concatenate([qt_ref[qi * ppb + j, h * HEAD_DIM:(h + 1) * HEAD_DIM, :]
                                          for j in range(qp0, qp1)], axis=1)
                    zq = jnp.zeros((HEAD_DIM, q_len), BF16)
                    qtz = jnp.concatenate([qh, zq] if h % 2 == 0 else [zq, qh], axis=0)
                    cs = crep_scr[h, pl.ds(k0, k_len), :]
                    sh = _nn(k_ref[pl.ds(k0, k_len), pair], qtz) - jnp.concatenate([cs] * (qp1 - qp0), axis=1)
                    if masked:
                        causal = (lax.broadcasted_iota(jnp.int32, (k_len, q_len), 0)
                                  <= lax.broadcasted_iota(jnp.int32, (k_len, q_len), 1))
                        sh = jnp.where(causal, sh, -jnp.inf)
                    return sh

                def softmax(sh, part, h):
                    qp0, qp1 = part[1]
                    lanes = slice(qp0 * PAGE, qp1 * PAGE)
                    ct = jnp.concatenate([ct_scr[qi * ppb + j, h:h + 1, :] for j in range(qp0, qp1)], axis=1)
                    m_row, l_row = m_scr[h], l_scr[h]
                    m = m_row[:, lanes]
                    m_new = jnp.maximum(m, jnp.max(sh, axis=0, keepdims=True) + ct)
                    a = jnp.exp2(m - m_new)
                    p = jnp.exp2(sh + (ct - m_new))
                    l_new = a * l_row[:, lanes] + jnp.sum(p, axis=0, keepdims=True)
                    def put(row, new):
                        parts = ([row[:, :lanes.start]] if lanes.start else []) + [new]
                        parts += [row[:, lanes.stop:]] if lanes.stop < tq else []
                        return jnp.concatenate(parts, axis=1)

                    m_scr[h] = put(m_row, m_new)
                    l_scr[h] = put(l_row, l_new)
                    return a, p.astype(BF16)

                def weighted_values(kj, part, h, a, p):
                    (kp0, kp1), (qp0, qp1), _ = part
                    lanes = slice(qp0 * PAGE, qp1 * PAGE)
                    vt = jnp.concatenate([vt_ref[kj * ppb + j, h * HEAD_DIM:(h + 1) * HEAD_DIM, :]
                                          for j in range(kp0, kp1)], axis=1)
                    acc_scr[h, :, lanes] = acc_scr[h, :, lanes] * a + _nn(vt, p)

                todo = [(kj, part, h) for kj, part in blocks for h in heads]
                for i in range(len(todo) + 2):
                    if i < len(todo):
                        s[i] = scores(*todo[i])
                    if 0 <= i - 2 < len(todo):
                        weighted_values(*todo[i - 2], alpha.pop(i - 2), pe.pop(i - 2))
                    if 0 <= i - 1 < len(todo):
                        alpha[i - 1], pe[i - 1] = softmax(s.pop(i - 1), *todo[i - 1][1:])

            full = ((0, ppb), (0, ppb), False)
            if ppb % 2 == 0:
                diagonal = [((0, ppb // 2), (0, ppb), True), ((ppb // 2, ppb), (ppb // 2, ppb), True)]
            else:
                diagonal = [((0, ppb), (0, ppb), True)]

            def pair_step(kp, c):
                k_step([(2 * kp, full), (2 * kp + 1, full)])
                return c

            def tail_after_odd(_, c):
                k_step([(qi - 1, full)] + [(qi, part) for part in diagonal])
                return c

            def tail_after_even(_, c):
                k_step([(qi, part) for part in diagonal])
                return c

            lax.fori_loop(0, qi // 2, pair_step, 0)
            lax.fori_loop(0, qi % 2, tail_after_odd, 0)
            lax.fori_loop(0, 1 - qi % 2, tail_after_even, 0)
            for hp in range(h0 // 2, (h0 + heads_per_step) // 2):
                ot = jnp.concatenate([acc_scr[2 * hp + e] / l_scr[2 * hp + e] for e in range(2)], axis=0)
                o_ref[pl.ds(q0, tq), hp * 2 * HEAD_DIM:(hp + 1) * 2 * HEAD_DIM] = ot.T.astype(o_ref.dtype)
            return carry

        lax.fori_loop(0, t // tq, q_block, 0)


ATTN_HEADS_PER_STEP = 8


def _prompt_attn(qt, kb, vtb, lft_layers, *, batch, tq):
    n, _ = kb.shape
    t = n // batch
    npages = t // PAGE
    layer = lft_layers.shape[0] - 1
    assert t % tq == 0 and tq % PAGE == 0
    paged_spec = pl.BlockSpec((npages, ATTN_WIDTH, PAGE), lambda b: (b, 0, 0))
    return pl.pallas_call(
        functools.partial(_prompt_attn_body, tq=tq, heads_per_step=ATTN_HEADS_PER_STEP),
        grid=(batch,),
        in_specs=[paged_spec, pl.BlockSpec((t, ATTN_WIDTH), lambda b: (b, 0)), paged_spec,
                  pl.BlockSpec((None, npages, N_HEADS, PAGE), lambda b: (layer, b, 0, 0))],
        out_specs=pl.BlockSpec((t, ATTN_WIDTH), lambda b: (b, 0)),
        out_shape=jax.ShapeDtypeStruct((n, ATTN_WIDTH), BF16),
        scratch_shapes=[pltpu.VMEM((npages, N_HEADS, PAGE), F32), pltpu.VMEM((t, LANES), F32),
                        pltpu.VMEM((N_HEADS, t, LANES), F32), pltpu.VMEM((N_HEADS, 1, tq), F32),
                        pltpu.VMEM((N_HEADS, 1, tq), F32), pltpu.VMEM((N_HEADS, HEAD_DIM, tq), F32)],
        compiler_params=pltpu.CompilerParams(dimension_semantics=("arbitrary",), vmem_limit_bytes=VMEM_LIMIT),
        name="prompt_attn",
    )(qt, kb, vtb, lft_layers)


def _bias_rows(r_scr, blk, tn):
    return jnp.concatenate(
        [jnp.broadcast_to(r_scr[blk * N_HEADS + h:blk * N_HEADS + h + 1, :], (tn, PAGE)) for h in range(N_HEADS)],
        axis=0)


def _sample_setup(q, lft_ref, lfn, r_scr, npages):
    tn = q.shape[0]
    hq = N_HEADS * tn
    row = lax.broadcasted_iota(jnp.int32, (hq, ATTN_WIDTH), 0)
    lane = lax.broadcasted_iota(jnp.int32, (hq, ATTN_WIDTH), 1)
    own_head = (lane // HEAD_DIM) == (row // tn)
    qbd = jnp.where(own_head, jnp.concatenate([q] * N_HEADS, axis=0), 0.0).astype(BF16)
    nblk = npages + 1
    x = jnp.concatenate([lft_ref[p] for p in range(npages)] + [lfn], axis=0)
    rw = _exact_nn(x, _tri(PAGE, lambda r, c: r > c))
    tot = rw[:, 0:1] + x[:, 0:1]
    run = jnp.zeros((N_HEADS, 1), F32)
    for p in reversed(range(nblk)):
        hs = slice(p * N_HEADS, (p + 1) * N_HEADS)
        r_scr[hs, :] = rw[hs, :] + run
        run = run + tot[hs, :]
    row_p = lax.broadcasted_iota(jnp.int32, (hq, PAGE), 0) % tn
    lane_p = lax.broadcasted_iota(jnp.int32, (hq, PAGE), 1)
    b_new = _bias_rows(r_scr, npages, tn)
    s_q = jnp.sum(jnp.where(lane_p == row_p, b_new, 0.0), axis=-1, keepdims=True)
    return qbd, own_head, b_new, s_q, lane_p <= row_p


def _sample_scores(qbd, s_q, k_pages, page0, r_scr, s_scr, m_prev, new):
    hq = qbd.shape[0]
    tn = hq // N_HEADS
    hp = k_pages.shape[0]
    m = m_prev
    for pp in range(0, hp, 2):
        kt2 = jnp.concatenate([k_pages[pp], k_pages[pp + 1]], axis=1).astype(BF16)
        bias = jnp.concatenate([_bias_rows(r_scr, page0 + pp, tn), _bias_rows(r_scr, page0 + pp + 1, tn)], axis=1)
        s = _nn(qbd, kt2) + bias - s_q
        s_scr[:, pp * PAGE:(pp + 2) * PAGE] = s
        m = jnp.maximum(m, jnp.max(s, axis=-1, keepdims=True))
    if new is not None:
        kn, vn, b_new, causal = new
        s = jnp.where(causal, _nt(qbd, kn) + b_new - s_q, -jnp.inf)
        s_scr[:, hp * PAGE:(hp + 1) * PAGE] = s
        m = jnp.maximum(m, jnp.max(s, axis=-1, keepdims=True))
    return m


def _sample_accumulate(v_pages, s_scr, state, m, new):
    m_prev, l_prev, acc_prev = state
    hp = v_pages.shape[0]
    alpha = jnp.exp(m_prev - m)
    l = alpha * l_prev
    acc = alpha * acc_prev
    for pp in range(0, hp, 2):
        pe = jnp.exp(s_scr[:, pp * PAGE:(pp + 2) * PAGE] - m)
        l = l + jnp.sum(pe, axis=-1, keepdims=True)
        vt2 = jnp.concatenate([v_pages[pp], v_pages[pp + 1]], axis=1).astype(BF16)
        acc = acc + _nt(pe.astype(BF16), vt2)
    if new is not None:
        vn = new[1]
        pe = jnp.exp(s_scr[:, hp * PAGE:(hp + 1) * PAGE] - m)
        l = l + jnp.sum(pe, axis=-1, keepdims=True)
        acc = acc + _nn(pe.astype(BF16), vn)
    return m, l, acc


FFN_CHUNK = 256
SEQ_HALVES = 2
RING_SLOTS = 3
PAGE_DMA_PRIORITY = 1


def _outproj_math(x_ref, za_ref, zb_ref, oc_ref, gate_ref, wa_ref, wb_ref, wc_ref, wo_ref, g2_ref,
                  wg_ref, wu_ref, wd_ref, gf_ref, y_ref, *, final, before_chunk=None, after_chunk=None):
    d = x_ref.shape[1]
    ya = _nn(za_ref[...].astype(BF16), wa_ref[...])
    yb = _nn(zb_ref[...].astype(BF16), wb_ref[...])
    yc = _nn(oc_ref[...].astype(BF16), wc_ref[...])
    merged = (gate_ref[:, 0:d].astype(F32) * ya + gate_ref[:, d:2 * d].astype(F32) * yb
              + gate_ref[:, 2 * d:3 * d].astype(F32) * yc)
    x1 = x_ref[...] + _nn(merged.astype(BF16), wo_ref[...])
    h2 = _rms(x1, g2_ref[...]).astype(BF16)
    acc = x1
    for k, c in enumerate(range(0, wg_ref.shape[1], FFN_CHUNK)):
        if before_chunk is not None:
            before_chunk(k)
        gch = _nn(h2, wg_ref[:, c:c + FFN_CHUNK])
        uch = _nn(h2, wu_ref[:, c:c + FFN_CHUNK])
        acc = acc + _nn((gch * _sigmoid(gch) * uch).astype(BF16), wd_ref[c:c + FFN_CHUNK, :])
        if after_chunk is not None:
            after_chunk(k)
    y_ref[...] = _rms(acc, gf_ref[...]) if final else acc


def _outproj_body(*refs, final):
    _outproj_math(*refs, final=final)


def _outproj_attn_body(pt_ref, *refs, final, layer, npages, nseq):
    proj_refs = refs[:14]
    q_ref, kn_ref, vn_ref, lfn_ref, ckt_hbm, cvt_hbm, clft_hbm, y_ref, os_ref = refs[14:23]
    kring, vring, lring, sems, s_scr, r_scr = refs[23:]
    i = pl.program_id(0)
    sps, tn, _ = q_ref.shape
    hp = npages // SEQ_HALVES
    hq = N_HEADS * tn

    def slot_of(seq, c):
        return (seq * SEQ_HALVES + c) % RING_SLOTS

    def copies(seq, c, slot, parity):
        out = []
        if c == 0:
            for p in range(npages):
                pid = pt_ref[seq * npages + p]
                out.append(pltpu.make_async_copy(clft_hbm.at[layer, pid], lring.at[parity, p], sems.at[parity, 2]))
        for p in range(hp):
            pid = pt_ref[seq * npages + c * hp + p]
            out.append(pltpu.make_async_copy(ckt_hbm.at[layer, pid], kring.at[slot, p], sems.at[slot, 0]))
        for p in range(hp):
            pid = pt_ref[seq * npages + c * hp + p]
            out.append(pltpu.make_async_copy(cvt_hbm.at[layer, pid], vring.at[slot, p], sems.at[slot, 1]))
        return out

    def fetch_after(j, c):
        seq = i * sps + j
        ahead = c + RING_SLOTS
        tgt_seq = jnp.minimum(seq + ahead // SEQ_HALVES, nseq - 1)
        return copies(tgt_seq, ahead % SEQ_HALVES, slot_of(seq, c), (j + ahead // SEQ_HALVES) % 2)

    @pl.when(i == 0)
    def _():
        for g in range(RING_SLOTS):
            for cp in copies(g // SEQ_HALVES, g % SEQ_HALVES, g, (g // SEQ_HALVES) % 2):
                cp.start(priority=PAGE_DMA_PRIORITY)

    items = [(j, c) for j in range(sps) for c in range(SEQ_HALVES)]
    n_ffn = -(-proj_refs[10].shape[1] // FFN_CHUNK)
    slots = [t * n_ffn // len(items) for t in range(len(items))]
    seq_state = {}
    scored = {}

    def score_phase(j, c):
        seq = i * sps + j
        parity = j % 2
        slot = slot_of(seq, c)
        for cp in copies(seq, c, slot, parity):
            cp.wait()
        if c == 0:
            qbd, own_head, b_new, s_q, causal = _sample_setup(q_ref[j], lring.at[parity], lfn_ref[j], r_scr, npages)
            state = (jnp.full((hq, 1), -jnp.inf, F32), jnp.zeros((hq, 1), F32), jnp.zeros((hq, ATTN_WIDTH), F32))
            new = None
        else:
            qbd, own_head, b_new, s_q, causal, state = seq_state.pop(j)
            zrows = jnp.zeros((PAGE - tn, ATTN_WIDTH), F32)
            kn = jnp.concatenate([kn_ref[j], zrows], axis=0).astype(BF16)
            vn = jnp.concatenate([vn_ref[j], zrows], axis=0).astype(BF16)
            new = (kn, vn, b_new, causal)
        m = _sample_scores(qbd, s_q, kring.at[slot], c * hp, r_scr, s_scr, state[0], new)
        scored[(j, c)] = (qbd, own_head, b_new, s_q, causal, state, m, new, slot)

    def accumulate_phase(j, c):
        seq = i * sps + j
        qbd, own_head, b_new, s_q, causal, state, m, new, slot = scored.pop((j, c))
        state = _sample_accumulate(vring.at[slot], s_scr, state, m, new)
        for cp in fetch_after(j, c):
            cp.start(priority=PAGE_DMA_PRIORITY)
        if c == 0:
            seq_state[j] = (qbd, own_head, b_new, s_q, causal, state)
        else:
            _, l, acc = state
            o = jnp.where(own_head, acc / l, 0.0)
            out = o[0:tn, :]
            for h in range(1, N_HEADS):
                out = out + o[h * tn:(h + 1) * tn, :]
            os_ref[j] = out

    def before_chunk(k):
        first = [t for t in range(len(items)) if slots[t] == k][:1]
        for t in first:
            score_phase(*items[t])

    def after_chunk(k):
        mine = [t for t in range(len(items)) if slots[t] == k]
        for n_t, t in enumerate(mine):
            if n_t > 0:
                score_phase(*items[t])
            accumulate_phase(*items[t])

    _outproj_math(*proj_refs, y_ref, final=final, before_chunk=before_chunk, after_chunk=after_chunk)

    @pl.when(i == pl.num_programs(0) - 1)
    def _():
        for j, c in items[-RING_SLOTS:]:
            for cp in fetch_after(j, c):
                cp.wait()


def _outproj_weights(w, gf, layer):
    const = lambda a: (a, pl.BlockSpec(a.shape, lambda *_: (0,) * a.ndim, pipeline_mode=pl.Buffered(1)))
    of_layer = lambda a: (a, pl.BlockSpec((None,) + a.shape[1:], lambda *_: (layer,) + (0,) * (a.ndim - 1),
                                          pipeline_mode=pl.Buffered(1)))
    return [const(w['wa']), const(w['wb']), const(w['wc']), const(w['wo']), const(w['norm2_g']),
            of_layer(w['wg_layers']), of_layer(w['wu_layers']), of_layer(w['wd_layers']), const(gf)]


def _outproj_attn(x2d, za, zb, oc, gates, w, gf, layer, page_table_flat, q, kn, vn, lfn, cache_kt, cache_vt,
                  cache_lft, *, final, tm, npages):
    n, d = x2d.shape
    nseq, tn, _ = q.shape
    steps = n // tm
    assert n % tm == 0 and nseq % steps == 0 and npages % (2 * SEQ_HALVES) == 0
    sps = nseq // steps
    assert sps % 2 == 0
    hp = npages // SEQ_HALVES
    assert nseq * SEQ_HALVES >= RING_SLOTS
    row = lambda a: pl.BlockSpec((tm, a.shape[1]), lambda i, pt: (i, 0))
    seqs = lambda width: pl.BlockSpec((sps, tn, width), lambda i, pt: (i, 0, 0))
    seqs_l = lambda width: pl.BlockSpec((None, sps, tn, width), lambda i, pt: (layer, i, 0, 0))
    hbm = pl.BlockSpec(memory_space=pl.ANY)
    weights = _outproj_weights(w, gf, layer)
    grid_spec = pltpu.PrefetchScalarGridSpec(
        num_scalar_prefetch=1,
        grid=(steps,),
        in_specs=[row(x2d), row(za), row(zb), row(oc), row(gates)] + [s for _, s in weights]
        + [seqs(ATTN_WIDTH), seqs_l(ATTN_WIDTH), seqs_l(ATTN_WIDTH),
           pl.BlockSpec((sps, N_HEADS, PAGE), lambda i, pt: (i, 0, 0)), hbm, hbm, hbm],
        out_specs=[pl.BlockSpec((tm, d), lambda i, pt: (i, 0)), seqs(ATTN_WIDTH)],
        scratch_shapes=[pltpu.VMEM((RING_SLOTS, hp, ATTN_WIDTH, PAGE), F32),
                        pltpu.VMEM((RING_SLOTS, hp, ATTN_WIDTH, PAGE), F32),
                        pltpu.VMEM((2, npages, N_HEADS, PAGE), F32),
                        pltpu.SemaphoreType.DMA((RING_SLOTS, 3)),
                        pltpu.VMEM((N_HEADS * tn, (hp + 1) * PAGE), F32),
                        pltpu.VMEM(((npages + 1) * N_HEADS, PAGE), F32)],
    )
    return pl.pallas_call(
        functools.partial(_outproj_attn_body, final=final, layer=layer, npages=npages, nseq=nseq),
        grid_spec=grid_spec,
        out_shape=[jax.ShapeDtypeStruct((n, d), F32), jax.ShapeDtypeStruct((nseq, tn, ATTN_WIDTH), F32)],
        compiler_params=pltpu.CompilerParams(dimension_semantics=("arbitrary",), vmem_limit_bytes=VMEM_LIMIT),
        name="outproj_attn_final" if final else "outproj_attn",
    )(page_table_flat, x2d, za, zb, oc, gates, *[a for a, _ in weights], q, kn, vn, lfn, cache_kt, cache_vt, cache_lft)


def _outproj(x2d, za, zb, oc, gates, w, gf, layer, *, final, tm):
    n, d = x2d.shape
    assert n % tm == 0
    row = lambda a: pl.BlockSpec((tm, a.shape[1]), lambda i: (i, 0))
    weights = _outproj_weights(w, gf, layer)
    return pl.pallas_call(
        functools.partial(_outproj_body, final=final),
        grid=(n // tm,),
        in_specs=[row(x2d), row(za), row(zb), row(oc), row(gates)] + [s for _, s in weights],
        out_specs=pl.BlockSpec((tm, d), lambda i: (i, 0)),
        out_shape=jax.ShapeDtypeStruct((n, d), F32),
        compiler_params=pltpu.CompilerParams(dimension_semantics=("arbitrary",), vmem_limit_bytes=VMEM_LIMIT),
        name="outproj_final" if final else "outproj",
    )(x2d, za, zb, oc, gates, *[a for a, _ in weights])


def _layer_weights(l, norm1_g, w_in, b_f, conv_a_w, conv_b_w, conv_b_bias, cf_norm_g, cf_norm_b,
                   w_a_out, w_b_out, w_c_out, w_o, norm2_g, w_ffn_gate, w_ffn_up, w_ffn_down):
    d = w_in.shape[1]
    wi_t = w_in[l].T
    off_f = 3 * ATTN_WIDTH
    off_mix = off_f + N_HEADS
    off_gate = off_mix + 3 * SC_WIDTH + 2 * CF_WIDTH
    wq_t, wk_t, wv_t = (wi_t[i * ATTN_WIDTH:(i + 1) * ATTN_WIDTH].astype(BF16) for i in range(3))
    pad = LANES - N_HEADS
    return {
        'norm1_g': norm1_g[l].reshape(1, d),
        'wq_t': wq_t, 'wk_t': wk_t, 'wv_t': wv_t,
        'wf_t': jnp.pad(wi_t[off_f:off_mix], ((0, pad), (0, 0))).astype(BF16),
        'bf': jnp.pad(b_f[l], (0, pad)).reshape(1, LANES),
        'wmix_t': wi_t[off_mix:off_gate].astype(BF16),
        'wgate_t': wi_t[off_gate:].astype(BF16),
        'conv_a_w': conv_a_w[l], 'conv_b_w': conv_b_w[l], 'conv_b_bias': conv_b_bias[l].reshape(1, -1),
        'cf_norm_g': cf_norm_g[l].reshape(1, -1), 'cf_norm_b': cf_norm_b[l].reshape(1, -1),
        'wa': w_a_out[l].astype(BF16), 'wb': w_b_out[l].astype(BF16), 'wc': w_c_out[l].astype(BF16),
        'wo': w_o[l].astype(BF16), 'norm2_g': norm2_g[l].reshape(1, d),
    }


def kernel(x_prompt, x_sample, cache_k, cache_v, cache_logf, state_conv_a, state_conv_b, page_table, norm1_g, w_in, b_f, conv_a_w, conv_b_w, conv_b_bias, cf_norm_g, cf_norm_b, w_a_out, w_b_out, w_c_out, w_o, norm2_g, w_ffn_gate, w_ffn_up, w_ffn_down, final_norm_g):
    depth = w_in.shape[0]
    pb, ps, d = x_prompt.shape
    sb_, st, _ = x_sample.shape
    n_pool = cache_k.shape[1]
    npages = page_table.shape[1]
    assert cache_k.shape[2:] == (PAGE, N_HEADS, HEAD_DIM) and ps % PAGE == 0
    cache_kt = cache_k.transpose(0, 1, 3, 4, 2).reshape(depth, n_pool, ATTN_WIDTH, PAGE)
    cache_vt = cache_v.transpose(0, 1, 3, 4, 2).reshape(depth, n_pool, ATTN_WIDTH, PAGE)
    cache_lft = cache_logf.transpose(0, 1, 3, 2)
    pt_flat = page_table.reshape(-1)
    gf = final_norm_g.reshape(1, d)
    tm_p = 512 if ps % 512 == 0 else PAGE
    tm_s = 512 if (sb_ * st) % 512 == 0 else sb_ * st
    tq = 256 if ps % 256 == 0 else PAGE
    bb = 16 if sb_ % 16 == 0 else sb_
    conv_rows = 64 if ps % 64 == 0 else ps

    xp = x_prompt.reshape(pb * ps, d)
    xs = x_sample.reshape(sb_ * st, d)
    outs = {k: [] for k in ('ap', 'bp', 'fs', 'as', 'bs')}
    prev_p = prev_s = None
    npp = ps // PAGE
    for l in range(depth):
        w = _layer_weights(l, norm1_g, w_in, b_f, conv_a_w, conv_b_w, conv_b_bias, cf_norm_g, cf_norm_b,
                           w_a_out, w_b_out, w_c_out, w_o, norm2_g, w_ffn_gate, w_ffn_up, w_ffn_down)
        final = l == depth - 1
        o_s = _inproj(xs, w, prev_s, paged=False, tm=tm_s)
        prev_s = (o_s['k'], o_s['v'])
        za_s, zb_s, na_s, nb_s = _mixers(o_s['ua'].reshape(sb_, st, -1), o_s['ub'].reshape(sb_, st, -1),
                                         o_s['sb'].reshape(sb_, st, -1), state_conv_a[l], state_conv_b[l], w,
                                         bb=bb, rows=st)
        lf = o_s['lf'][:, :N_HEADS].reshape(sb_, st, N_HEADS)
        lfn = jnp.pad(lf.transpose(0, 2, 1), ((0, 0), (0, 0), (0, PAGE - st)))
        casts = {'wg': w_ffn_gate, 'wu': w_ffn_up, 'wd': w_ffn_down} if l == 0 else None
        o_p = _inproj(xp, w, prev_p, paged=True, tm=tm_p, tiles_per_seq=ps // tm_p, conv_rows=conv_rows, casts=casts)
        if l == 0:
            ffn = {name + '_layers': o_p[name + '_bf16'] for name in casts}
        w.update(ffn)
        prev_p = (o_p['k'], o_p['v'], o_p['lf'])
        oc = _prompt_attn(o_p['q'], o_p['kb'], o_p['vb'], o_p['lf'], batch=pb, tq=tq)
        xp, oc_s = _outproj_attn(xp, o_p['za'].reshape(pb * ps, -1), o_p['zb'].reshape(pb * ps, -1), oc, o_p['gate'],
                                 w, gf, l, pt_flat, o_s['q'].reshape(sb_, st, -1),
                                 o_s['k'].reshape(l + 1, sb_, st, -1), o_s['v'].reshape(l + 1, sb_, st, -1),
                                 lfn, cache_kt, cache_vt, cache_lft, final=final, tm=tm_p, npages=npages)
        xs = _outproj(xs, za_s.reshape(sb_ * st, -1), zb_s.reshape(sb_ * st, -1), oc_s.reshape(sb_ * st, -1),
                      o_s['gate'], w, gf, l, final=final, tm=tm_s)
        outs['ap'].append(o_p['na'])
        outs['bp'].append(o_p['nb'])
        outs['fs'].append(lf)
        outs['as'].append(na_s)
        outs['bs'].append(nb_s)
    stack = lambda k: jnp.stack(outs[k])
    kt, vt, lft = prev_p
    to_pages = lambda a: a.reshape(depth, pb, npp, N_HEADS, HEAD_DIM, PAGE).transpose(0, 1, 2, 5, 3, 4)
    ks, vs = (a.reshape(depth, sb_, st, N_HEADS, HEAD_DIM) for a in prev_s)
    return (xp.reshape(pb, ps, d), xs.reshape(sb_, st, d), to_pages(kt), to_pages(vt),
            lft.reshape(depth, pb, npp, N_HEADS, PAGE).transpose(0, 1, 2, 4, 3), stack('ap'), stack('bp'),
            ks, vs, stack('fs'), stack('as'), stack('bs'))
```
